```python
import jax, jax.numpy as jnp
from jax import lax
import numpy as np

D_MODEL = 2048
BATCH = 1
SEQ = 8192
DEPTH = 1

HEAD_DIM = 128
RET_HEADS = (D_MODEL // 2) // HEAD_DIM
ATT_Q_HEADS = (D_MODEL // 2) // HEAD_DIM
ATT_KV_HEADS = ATT_Q_HEADS // 4
RET_WIDTH = RET_HEADS * HEAD_DIM
ATT_WIDTH = ATT_Q_HEADS * HEAD_DIM
KV_WIDTH = ATT_KV_HEADS * HEAD_DIM
MIX_WIDTH = RET_WIDTH + ATT_WIDTH
IN_WIDTH = 4 * RET_WIDTH + ATT_WIDTH + 2 * KV_WIDTH
D_FF = 256 * ((8 * D_MODEL // 3 + 255) // 256)
CHUNK = 128
WINDOW = 128
ROPE_THETA = 500000.0
ROPE_DIM = HEAD_DIM // 4
RET_THETA = 10000.0
EPS = 1e-6

kernel_name = "hybrid_retention_swa_sink_macaron"


def rms_norm(x, g):
    xf = x.astype(jnp.float32)
    y = xf * lax.rsqrt(jnp.mean(xf * xf, axis=-1, keepdims=True) + EPS)
    return (y * g.astype(jnp.float32)).astype(x.dtype)


def rope_tables(seq_len, dim, theta):
    inv = theta ** (-jnp.arange(0, dim, 2, dtype=jnp.float32) / dim)
    ang = jnp.arange(seq_len, dtype=jnp.float32)[:, None] * inv[None, :]
    ang = jnp.concatenate([ang, ang], axis=-1)
    return jnp.cos(ang), jnp.sin(ang)


def apply_rotary(x, cos, sin):
    c = cos[None, :, None, :].astype(x.dtype)
    s = sin[None, :, None, :].astype(x.dtype)
    x1, x2 = jnp.split(x, 2, axis=-1)
    return x * c + jnp.concatenate([-x2, x1], axis=-1) * s


def swiglu(h, w_gate, w_up, w_down):
    return (jax.nn.silu(h @ w_gate) * (h @ w_up)) @ w_down


def retention(q, k, v):
    B, S, H, d = q.shape
    dv = v.shape[-1]
    nc = S // CHUNK
    q = q.astype(jnp.float32)
    k = k.astype(jnp.float32) * (d ** -0.5)
    v = v.astype(jnp.float32)
    log_g = jnp.log1p(-jnp.exp2(-5.0 - jnp.arange(H, dtype=jnp.float32)))
    pos = jnp.arange(CHUNK, dtype=jnp.float32)
    diff = pos[:, None] - pos[None, :]
    intra_decay = jnp.where(diff[None] >= 0,
                            jnp.exp(jnp.maximum(diff, 0.0)[None] * log_g[:, None, None]), 0.0)
    qc = q.reshape(B, nc, CHUNK, H, d)
    kc = k.reshape(B, nc, CHUNK, H, d)
    vc = v.reshape(B, nc, CHUNK, H, dv)
    scores = jnp.einsum('bcnhd,bcmhd->bchnm', qc, kc) * intra_decay
    intra = jnp.einsum('bchnm,bcmhe->bcnhe', scores, vc)
    k_tail = jnp.exp((CHUNK - 1.0 - pos)[:, None] * log_g[None, :])
    chunk_kv = jnp.einsum('bcmhd,bcmhe->bchde', kc * k_tail[:, :, None], vc)
    chunk_decay = jnp.exp(CHUNK * log_g)[None, :, None, None]

    def step(state, kv):
        return state * chunk_decay + kv, state

    _, prev = lax.scan(step, jnp.zeros((B, H, d, dv), jnp.float32), jnp.moveaxis(chunk_kv, 1, 0))
    prev = jnp.moveaxis(prev, 0, 1)
    q_head = jnp.exp((pos + 1.0)[:, None] * log_g[None, :])
    cross = jnp.einsum('bcnhd,bchde->bcnhe', qc * q_head[:, :, None], prev)
    return (intra + cross).reshape(B, S, H, dv)


def sliding_window_sink_attention(q, k, v, sinks):
    B, S, Hq, d = q.shape
    Hkv = k.shape[2]
    G = Hq // Hkv
    nb = S // WINDOW
    qb = q.reshape(B, nb, WINDOW, Hkv, G, d)
    kb = k.reshape(B, nb, WINDOW, Hkv, d)
    vb = v.reshape(B, nb, WINDOW, Hkv, d)
    shift = lambda t: jnp.concatenate([jnp.zeros_like(t[:, :1]), t[:, :-1]], axis=1)
    kk = jnp.concatenate([shift(kb), kb], axis=2)
    vv = jnp.concatenate([shift(vb), vb], axis=2)
    s = jnp.einsum('bnqhgd,bnkhd->bnhgqk', qb, kk).astype(jnp.float32) * (d ** -0.5)
    blk = jnp.arange(nb)[:, None, None]
    qi = jnp.arange(WINDOW)[None, :, None]
    kj = jnp.arange(2 * WINDOW)[None, None, :]
    rel = WINDOW + qi - kj
    kpos = (blk - 1) * WINDOW + kj
    mask = (rel >= 0) & (rel < WINDOW) & (kpos >= 0)
    s = jnp.where(mask[None, :, None, None], s, jnp.finfo(jnp.float32).min)
    sink_col = jnp.broadcast_to(sinks.astype(jnp.float32).reshape(1, 1, Hkv, G, 1, 1),
                                s.shape[:-1] + (1,))
    p = jax.nn.softmax(jnp.concatenate([s, sink_col], axis=-1), axis=-1)[..., :-1]
    o = jnp.einsum('bnhgqk,bnkhd->bnqhgd', p.astype(v.dtype), vv)
    return o.reshape(B, S, Hq * d)


def setup_inputs(seed: int = 0) -> dict:
    key = jax.random.key(seed)
    ks = jax.random.split(key, 16)
    f32 = jnp.float32
    nrm = lambda k, shape, scale: jax.random.normal(k, shape, f32) * scale
    gain = lambda k, shape: 1.0 + 0.02 * jax.random.normal(k, shape, f32)
    return {
        "x": jax.random.normal(ks[0], (BATCH, SEQ, D_MODEL), f32),
        "ffn1_norm": gain(ks[1], (DEPTH, D_MODEL)),
        "ffn1_w_gate": nrm(ks[2], (DEPTH, D_MODEL, D_FF), D_MODEL ** -0.5),
        "ffn1_w_up": nrm(ks[3], (DEPTH, D_MODEL, D_FF), D_MODEL ** -0.5),
        "ffn1_w_down": nrm(ks[4], (DEPTH, D_FF, D_MODEL), D_FF ** -0.5),
        "mix_norm": gain(ks[5], (DEPTH, D_MODEL)),
        "w_in": nrm(ks[6], (DEPTH, D_MODEL, IN_WIDTH), D_MODEL ** -0.5),
        "q_norm": gain(ks[7], (DEPTH, HEAD_DIM)),
        "k_norm": gain(ks[8], (DEPTH, HEAD_DIM)),
        "attn_sinks": nrm(ks[9], (DEPTH, ATT_Q_HEADS), 1.0),
        "w_out": nrm(ks[10], (DEPTH, MIX_WIDTH, D_MODEL), MIX_WIDTH ** -0.5),
        "ffn2_norm": gain(ks[11], (DEPTH, D_MODEL)),
        "ffn2_w_gate": nrm(ks[12], (DEPTH, D_MODEL, D_FF), D_MODEL ** -0.5),
        "ffn2_w_up": nrm(ks[13], (DEPTH, D_MODEL, D_FF), D_MODEL ** -0.5),
        "ffn2_w_down": nrm(ks[14], (DEPTH, D_FF, D_MODEL), D_FF ** -0.5),
    }


def reference(x, ffn1_norm, ffn1_w_gate, ffn1_w_up, ffn1_w_down, mix_norm, w_in, q_norm, k_norm,
              attn_sinks, w_out, ffn2_norm, ffn2_w_gate, ffn2_w_up, ffn2_w_down):
    B, S, _ = x.shape
    cos_a, sin_a = rope_tables(S, ROPE_DIM, ROPE_THETA)
    cos_r, sin_r = rope_tables(S, HEAD_DIM, RET_THETA)
    split_at = [RET_WIDTH, 2 * RET_WIDTH, 3 * RET_WIDTH, 4 * RET_WIDTH,
                4 * RET_WIDTH + ATT_WIDTH, 4 * RET_WIDTH + ATT_WIDTH + KV_WIDTH]
    for l in range(DEPTH):
        h = rms_norm(x, ffn1_norm[l])
        x = x + 0.5 * swiglu(h, ffn1_w_gate[l], ffn1_w_up[l], ffn1_w_down[l])

        h = rms_norm(x, mix_norm[l])
        z = h @ w_in[l]
        q_r, k_r, v_r, g_r, q_a, k_a, v_a = jnp.split(z, split_at, axis=-1)

        q_r = apply_rotary(q_r.reshape(B, S, RET_HEADS, HEAD_DIM), cos_r, sin_r)
        k_r = apply_rotary(k_r.reshape(B, S, RET_HEADS, HEAD_DIM), cos_r, sin_r)
        v_r = v_r.reshape(B, S, RET_HEADS, HEAD_DIM)
        y_r = retention(q_r, k_r, v_r)
        y_r = y_r * lax.rsqrt(jnp.mean(y_r * y_r, axis=-1, keepdims=True) + EPS)
        y_r = (jax.nn.silu(g_r.astype(jnp.float32)) * y_r.reshape(B, S, RET_WIDTH)).astype(x.dtype)

        q_a = rms_norm(q_a.reshape(B, S, ATT_Q_HEADS, HEAD_DIM), q_norm[l])
        k_a = rms_norm(k_a.reshape(B, S, ATT_KV_HEADS, HEAD_DIM), k_norm[l])
        q_a = jnp.concatenate([apply_rotary(q_a[..., :ROPE_DIM], cos_a, sin_a), q_a[..., ROPE_DIM:]], axis=-1)
        k_a = jnp.concatenate([apply_rotary(k_a[..., :ROPE_DIM], cos_a, sin_a), k_a[..., ROPE_DIM:]], axis=-1)
        v_a = v_a.reshape(B, S, ATT_KV_HEADS, HEAD_DIM)
        y_a = sliding_window_sink_attention(q_a, k_a, v_a, attn_sinks[l])

        x = x + jnp.concatenate([y_r, y_a.astype(x.dtype)], axis=-1) @ w_out[l]

        h = rms_norm(x, ffn2_norm[l])
        x = x + 0.5 * swiglu(h, ffn2_w_gate[l], ffn2_w_up[l], ffn2_w_down[l])
    return x
```

```python
import functools

import jax
import jax.numpy as jnp
from jax import lax
from jax.experimental import pallas as pl
from jax.experimental.pallas import tpu as pltpu

D_MODEL = 2048
HEAD_DIM = 128
RET_HEADS = 8
ATT_Q_HEADS = 8
ATT_KV_HEADS = 2
GQA_GROUP = ATT_Q_HEADS // ATT_KV_HEADS
RET_WIDTH = RET_HEADS * HEAD_DIM
ATT_WIDTH = ATT_Q_HEADS * HEAD_DIM
KV_WIDTH = ATT_KV_HEADS * HEAD_DIM
MIX_WIDTH = RET_WIDTH + ATT_WIDTH
IN_WIDTH = 4 * RET_WIDTH + ATT_WIDTH + 2 * KV_WIDTH
CHUNK = 128
WINDOW = 128
ROPE_THETA = 500000.0
ROPE_DIM = HEAD_DIM // 4
RET_THETA = 10000.0
EPS = 1e-6

OFF_QR = 0
OFF_KR = RET_WIDTH
OFF_VR = 2 * RET_WIDTH
OFF_GR = 3 * RET_WIDTH
OFF_QA = 4 * RET_WIDTH
OFF_KA = OFF_QA + ATT_WIDTH
OFF_VA = OFF_KA + KV_WIDTH

VMEM_LIMIT_BYTES = 56 * 1024 * 1024

F32 = jnp.float32
BF16 = jnp.bfloat16


def _rms_rows(x, gain):
    ms = jnp.mean(x * x, axis=-1, keepdims=True)
    return x * lax.rsqrt(ms + EPS) * gain


def _ffn_kernel(x_ref, gain_ref, wg_ref, wu_ref, wd_ref, o_ref, h_ref, *, row_chunk):
    j = pl.program_id(1)
    tm = x_ref.shape[0]

    @pl.when(j == 0)
    def _():
        def body(r, carry):
            rows = pl.ds(pl.multiple_of(r * row_chunk, row_chunk), row_chunk)
            x = x_ref[rows, :]
            h_ref[rows, :] = _rms_rows(x, gain_ref[...]).astype(BF16)
            o_ref[rows, :] = x
            return carry

        lax.fori_loop(0, tm // row_chunk, body, 0)

    h = h_ref[...]
    g = jnp.dot(h, wg_ref[...], preferred_element_type=F32)
    u = jnp.dot(h, wu_ref[...], preferred_element_type=F32)
    a = (0.5 * (g * jax.nn.sigmoid(g)) * u).astype(BF16)
    o_ref[...] += jnp.dot(a, wd_ref[...], preferred_element_type=F32)


def _ffn(x, gain, wg, wu, wd, *, tm=1024, tf=512):
    s, d = x.shape
    d_ff = wg.shape[1]
    return pl.pallas_call(
        functools.partial(_ffn_kernel, row_chunk=128),
        name="ffn",
        grid=(s // tm, d_ff // tf),
        in_specs=[
            pl.BlockSpec((tm, d), lambda i, j: (i, 0)),
            pl.BlockSpec((1, d), lambda i, j: (0, 0)),
            pl.BlockSpec((d, tf), lambda i, j: (0, j)),
            pl.BlockSpec((d, tf), lambda i, j: (0, j)),
            pl.BlockSpec((tf, d), lambda i, j: (j, 0)),
        ],
        out_specs=pl.BlockSpec((tm, d), lambda i, j: (i, 0)),
        out_shape=jax.ShapeDtypeStruct((s, d), F32),
        scratch_shapes=[pltpu.VMEM((tm, d), BF16)],
        compiler_params=pltpu.CompilerParams(
            dimension_semantics=("parallel", "arbitrary"),
            vmem_limit_bytes=VMEM_LIMIT_BYTES,
        ),
    )(x, gain, wg, wu, wd)


def _in_proj_kernel(x_ref, gain_ref, w_ref, z_ref, h_ref, *, row_chunk):
    j = pl.program_id(1)
    tm = x_ref.shape[0]

    @pl.when(j == 0)
    def _():
        def body(r, carry):
            rows = pl.ds(pl.multiple_of(r * row_chunk, row_chunk), row_chunk)
            h_ref[rows, :] = _rms_rows(x_ref[rows, :], gain_ref[...]).astype(BF16)
            return carry

        lax.fori_loop(0, tm // row_chunk, body, 0)

    z_ref[...] = jnp.dot(h_ref[...], w_ref[...], preferred_element_type=F32)


def _in_proj(x, gain, w, *, tm=1024, tn=512):
    s, d = x.shape
    n = w.shape[1]
    return pl.pallas_call(
        functools.partial(_in_proj_kernel, row_chunk=128),
        name="in_proj",
        grid=(s // tm, n // tn),
        in_specs=[
            pl.BlockSpec((tm, d), lambda i, j: (i, 0)),
            pl.BlockSpec((1, d), lambda i, j: (0, 0)),
            pl.BlockSpec((d, tn), lambda i, j: (0, j)),
        ],
        out_specs=pl.BlockSpec((tm, tn), lambda i, j: (i, j)),
        out_shape=jax.ShapeDtypeStruct((s, n), F32),
        scratch_shapes=[pltpu.VMEM((tm, d), BF16)],
        compiler_params=pltpu.CompilerParams(
            dimension_semantics=("parallel", "arbitrary"),
            vmem_limit_bytes=VMEM_LIMIT_BYTES,
        ),
    )(x, gain, w)


def _dot_nt(a, b):
    return lax.dot_general(a, b, (((1,), (1,)), ((), ())), preferred_element_type=F32)


def _dot_tn(a, b):
    return lax.dot_general(a, b, (((0,), (0,)), ((), ())), preferred_element_type=F32)


def _mixer_kernel(sink_ref, cdecay_ref, z_ref, cos_r_ref, sin_r_ref, cos_a_ref, sin_lo_ref,
                  sin_hi_ref, qn_ref, kn_ref, decay_ref, qhead_ref, ktail_ref,
                  y_ref, state_ref, kprev_ref, vprev_ref):
    c = pl.program_id(0)

    @pl.when(c == 0)
    def _():
        state_ref[...] = jnp.zeros_like(state_ref)
        kprev_ref[...] = jnp.zeros_like(kprev_ref)
        vprev_ref[...] = jnp.zeros_like(vprev_ref)

    def head(off, h):
        return z_ref[:, off + h * HEAD_DIM: off + (h + 1) * HEAD_DIM]

    cos_r = cos_r_ref[...]
    sin_r = sin_r_ref[...]

    def rot_r(t):
        return t * cos_r + pltpu.roll(t, HEAD_DIM // 2, 1) * sin_r

    for h in range(RET_HEADS):
        q = rot_r(head(OFF_QR, h))
        k = rot_r(head(OFF_KR, h)) * (HEAD_DIM ** -0.5)
        vb = head(OFF_VR, h).astype(BF16)
        scores = _dot_nt(q.astype(BF16), k.astype(BF16)) * decay_ref[h]
        intra = jnp.dot(scores.astype(BF16), vb, preferred_element_type=F32)
        chunk_kv = _dot_tn((k * ktail_ref[h]).astype(BF16), vb)
        prev = state_ref[h]
        cross = jnp.dot((q * qhead_ref[h]).astype(BF16), prev.astype(BF16),
                        preferred_element_type=F32)
        state_ref[h] = prev * cdecay_ref[h] + chunk_kv
        y = intra + cross
        y = y * lax.rsqrt(jnp.mean(y * y, axis=-1, keepdims=True) + EPS)
        g = head(OFF_GR, h)
        y_ref[:, h * HEAD_DIM:(h + 1) * HEAD_DIM] = ((g * jax.nn.sigmoid(g)) * y).astype(BF16)

    cos_a = cos_a_ref[...]
    sin_lo = sin_lo_ref[...]
    sin_hi = sin_hi_ref[...]
    half = ROPE_DIM // 2

    def rot_a(t):
        return (t * cos_a + pltpu.roll(t, HEAD_DIM - half, 1) * sin_lo
                + pltpu.roll(t, half, 1) * sin_hi)

    qi = lax.broadcasted_iota(jnp.int32, (WINDOW, 2 * WINDOW), 0)
    kj = lax.broadcasted_iota(jnp.int32, (WINDOW, 2 * WINDOW), 1)
    rel = WINDOW + qi - kj
    mask = (rel >= 0) & (rel < WINDOW) & ((kj >= WINDOW) | (c > 0))
    neg = jnp.finfo(F32).min

    for kh in range(ATT_KV_HEADS):
        kb = rot_a(_rms_rows(head(OFF_KA, kh), kn_ref[...])).astype(BF16)
        vb = head(OFF_VA, kh).astype(BF16)
        kk = jnp.concatenate([kprev_ref[kh], kb], axis=0)
        vv = jnp.concatenate([vprev_ref[kh], vb], axis=0)
        for gq in range(GQA_GROUP):
            qh = kh * GQA_GROUP + gq
            qb = rot_a(_rms_rows(head(OFF_QA, qh), qn_ref[...])).astype(BF16)
            s = _dot_nt(qb, kk) * (HEAD_DIM ** -0.5)
            s = jnp.where(mask, s, neg)
            sink = sink_ref[qh]
            m = jnp.maximum(jnp.max(s, axis=-1, keepdims=True), sink)
            p = jnp.exp(s - m)
            denom = jnp.sum(p, axis=-1, keepdims=True) + jnp.exp(sink - m)
            p = p / denom
            o = jnp.dot(p.astype(BF16), vv, preferred_element_type=F32)
            col = RET_WIDTH + qh * HEAD_DIM
            y_ref[:, col:col + HEAD_DIM] = o.astype(BF16)
        kprev_ref[kh] = kb
        vprev_ref[kh] = vb


def _mixer(z, sinks, tables, q_gain, k_gain):
    s = z.shape[0]
    nc = s // CHUNK
    row_tab = pl.BlockSpec((CHUNK, HEAD_DIM), lambda c: (c, 0))
    whole = lambda shape: pl.BlockSpec(shape, lambda c: (0,) * len(shape))
    smem = pl.BlockSpec(memory_space=pltpu.SMEM)
    head_tab = (RET_HEADS, CHUNK, HEAD_DIM)
    return pl.pallas_call(
        _mixer_kernel,
        name="mixer",
        grid=(nc,),
        in_specs=[
            smem, smem,
            pl.BlockSpec((CHUNK, IN_WIDTH), lambda c: (c, 0)),
            row_tab, row_tab, row_tab, row_tab, row_tab,
            whole((1, HEAD_DIM)), whole((1, HEAD_DIM)),
            whole(head_tab), whole(head_tab), whole(head_tab),
        ],
        out_specs=pl.BlockSpec((CHUNK, MIX_WIDTH), lambda c: (c, 0)),
        out_shape=jax.ShapeDtypeStruct((s, MIX_WIDTH), BF16),
        scratch_shapes=[
            pltpu.VMEM((RET_HEADS, HEAD_DIM, HEAD_DIM), F32),
            pltpu.VMEM((ATT_KV_HEADS, WINDOW, HEAD_DIM), BF16),
            pltpu.VMEM((ATT_KV_HEADS, WINDOW, HEAD_DIM), BF16),
        ],
        compiler_params=pltpu.CompilerParams(
            dimension_semantics=("arbitrary",),
            vmem_limit_bytes=VMEM_LIMIT_BYTES,
        ),
    )(sinks, tables["chunk_decay"], z, tables["cos_r"], tables["sin_r"], tables["cos_a"],
      tables["sin_lo"], tables["sin_hi"], q_gain, k_gain, tables["decay"], tables["q_head"],
      tables["k_tail"])


def _out_proj_kernel(x_ref, y_ref, w_ref, o_ref):
    o_ref[...] = x_ref[...] + jnp.dot(y_ref[...], w_ref[...], preferred_element_type=F32)


def _out_proj(x, y, w, *, tm=512):
    s, d = x.shape
    k = y.shape[1]
    return pl.pallas_call(
        _out_proj_kernel,
        name="out_proj",
        grid=(s // tm,),
        in_specs=[
            pl.BlockSpec((tm, d), lambda i: (i, 0)),
            pl.BlockSpec((tm, k), lambda i: (i, 0)),
            pl.BlockSpec((k, d), lambda i: (0, 0)),
        ],
        out_specs=pl.BlockSpec((tm, d), lambda i: (i, 0)),
        out_shape=jax.ShapeDtypeStruct((s, d), F32),
        compiler_params=pltpu.CompilerParams(
            dimension_semantics=("parallel",),
            vmem_limit_bytes=VMEM_LIMIT_BYTES,
        ),
    )(x, y, w)


def _rope_tables(seq_len, dim, theta):
    inv = theta ** (-jnp.arange(0, dim, 2, dtype=F32) / dim)
    ang = jnp.arange(seq_len, dtype=F32)[:, None] * inv[None, :]
    ang = jnp.concatenate([ang, ang], axis=-1)
    return jnp.cos(ang), jnp.sin(ang)


def _position_tables(seq_len):
    cos_r, sin_r = _rope_tables(seq_len, HEAD_DIM, RET_THETA)
    hd2 = HEAD_DIM // 2
    sin_r = jnp.concatenate([-sin_r[:, :hd2], sin_r[:, hd2:]], axis=-1)

    cos_a, sin_a = _rope_tables(seq_len, ROPE_DIM, ROPE_THETA)
    half = ROPE_DIM // 2
    pad = lambda t, lo, fill: jnp.concatenate(
        [jnp.zeros((seq_len, lo), F32), t,
         jnp.full((seq_len, HEAD_DIM - lo - t.shape[1]), fill, F32)], axis=-1)
    cos_full = pad(cos_a, 0, 1.0)
    sin_lo = pad(-sin_a[:, :half], 0, 0.0)
    sin_hi = pad(sin_a[:, half:], half, 0.0)

    log_g = jnp.log1p(-jnp.exp2(-5.0 - jnp.arange(RET_HEADS, dtype=F32)))
    pos = jnp.arange(CHUNK, dtype=F32)
    diff = pos[:, None] - pos[None, :]
    decay = jnp.where(diff[None] >= 0,
                      jnp.exp(jnp.maximum(diff, 0.0)[None] * log_g[:, None, None]), 0.0)
    k_tail = jnp.exp((CHUNK - 1.0 - pos)[None, :] * log_g[:, None])
    q_head = jnp.exp((pos + 1.0)[None, :] * log_g[:, None])
    bcast = lambda t: jnp.broadcast_to(t[:, :, None], (RET_HEADS, CHUNK, HEAD_DIM))
    return {
        "cos_r": cos_r, "sin_r": sin_r, "cos_a": cos_full, "sin_lo": sin_lo, "sin_hi": sin_hi,
        "decay": decay, "k_tail": bcast(k_tail), "q_head": bcast(q_head),
        "chunk_decay": jnp.exp(CHUNK * log_g),
    }


def kernel(x, ffn1_norm, ffn1_w_gate, ffn1_w_up, ffn1_w_down, mix_norm, w_in, q_norm, k_norm,
           attn_sinks, w_out, ffn2_norm, ffn2_w_gate, ffn2_w_up, ffn2_w_down):
    b, s, d = x.shape
    depth = ffn1_norm.shape[0]
    tables = _position_tables(s)
    outs = []
    for bi in range(b):
        xb = x[bi]
        for l in range(depth):
            xb = _ffn(xb, ffn1_norm[l][None], ffn1_w_gate[l].astype(BF16),
                      ffn1_w_up[l].astype(BF16), ffn1_w_down[l].astype(BF16))
            z = _in_proj(xb, mix_norm[l][None], w_in[l].astype(BF16))
            y = _mixer(z, attn_sinks[l], tables, q_norm[l][None], k_norm[l][None])
            xb = _out_proj(xb, y, w_out[l].astype(BF16))
            xb = _ffn(xb, ffn2_norm[l][None], ffn2_w_gate[l].astype(BF16),
                      ffn2_w_up[l].astype(BF16), ffn2_w_down[l].astype(BF16))
        outs.append(xb)
    return jnp.stack(outs, axis=0)
```

```python
import functools

import jax
import jax.numpy as jnp
from jax import lax
from jax.experimental import pallas as pl
from jax.experimental.pallas import tpu as pltpu

D_MODEL = 2048
HEAD_DIM = 128
RET_HEADS = 8
ATT_Q_HEADS = 8
ATT_KV_HEADS = 2
GQA_GROUP = ATT_Q_HEADS // ATT_KV_HEADS
RET_WIDTH = RET_HEADS * HEAD_DIM
ATT_WIDTH = ATT_Q_HEADS * HEAD_DIM
KV_WIDTH = ATT_KV_HEADS * HEAD_DIM
MIX_WIDTH = RET_WIDTH + ATT_WIDTH
IN_WIDTH = 4 * RET_WIDTH + ATT_WIDTH + 2 * KV_WIDTH
CHUNK = 128
WINDOW = 128
ROPE_THETA = 500000.0
ROPE_DIM = HEAD_DIM // 4
RET_THETA = 10000.0
EPS = 1e-6

OFF_QR = 0
OFF_KR = RET_WIDTH
OFF_VR = 2 * RET_WIDTH
OFF_GR = 3 * RET_WIDTH
OFF_QA = 4 * RET_WIDTH
OFF_KA = OFF_QA + ATT_WIDTH
OFF_VA = OFF_KA + KV_WIDTH

VMEM_LIMIT_BYTES = 56 * 1024 * 1024

F32 = jnp.float32
BF16 = jnp.bfloat16


def _rms_rows(x, gain):
    ms = jnp.mean(x * x, axis=-1, keepdims=True)
    return x * lax.rsqrt(ms + EPS) * gain


def _ffn_kernel(x_hbm, gain_ref, wg_ref, wu_ref, wd_ref, o_ref, h_ref, sem, *, row_chunk):
    i = pl.program_id(0)
    j = pl.program_id(1)
    tm = o_ref.shape[0]
    n_chunks = tm // row_chunk

    @pl.when(j == 0)
    def _():
        def x_copy(r):
            return pltpu.make_async_copy(
                x_hbm.at[pl.ds(i * tm + r * row_chunk, row_chunk), :],
                o_ref.at[pl.ds(r * row_chunk, row_chunk), :],
                sem.at[r])

        for r in range(n_chunks):
            x_copy(r).start()
        for r in range(n_chunks):
            x_copy(r).wait()
            rows = pl.ds(r * row_chunk, row_chunk)
            h_ref[rows, :] = _rms_rows(o_ref[rows, :], gain_ref[...]).astype(BF16)

    h = h_ref[...]
    g = jnp.dot(h, wg_ref[...].astype(BF16), preferred_element_type=F32)
    u = jnp.dot(h, wu_ref[...].astype(BF16), preferred_element_type=F32)
    a = (0.5 * (g * jax.nn.sigmoid(g)) * u).astype(BF16)
    o_ref[...] += jnp.dot(a, wd_ref[...].astype(BF16), preferred_element_type=F32)


def _ffn(x, gain, wg, wu, wd, *, tm=1024, tf=512, row_chunk=128):
    s, d = x.shape
    d_ff = wg.shape[1]
    return pl.pallas_call(
        functools.partial(_ffn_kernel, row_chunk=row_chunk),
        name="ffn",
        grid=(s // tm, d_ff // tf),
        in_specs=[
            pl.BlockSpec(memory_space=pl.ANY),
            pl.BlockSpec((1, d), lambda i, j: (0, 0)),
            pl.BlockSpec((d, tf), lambda i, j: (0, j)),
            pl.BlockSpec((d, tf), lambda i, j: (0, j)),
            pl.BlockSpec((tf, d), lambda i, j: (j, 0)),
        ],
        out_specs=pl.BlockSpec((tm, d), lambda i, j: (i, 0)),
        out_shape=jax.ShapeDtypeStruct((s, d), F32),
        scratch_shapes=[pltpu.VMEM((tm, d), BF16), pltpu.SemaphoreType.DMA((tm // row_chunk,))],
        compiler_params=pltpu.CompilerParams(
            dimension_semantics=("parallel", "arbitrary"),
            vmem_limit_bytes=VMEM_LIMIT_BYTES,
        ),
    )(x, gain, wg, wu, wd)


def _in_proj_kernel(x_ref, gain_ref, w_ref, z_ref, h_ref, *, row_chunk):
    j = pl.program_id(1)
    tm = x_ref.shape[0]

    @pl.when(j == 0)
    def _():
        def body(r, carry):
            rows = pl.ds(pl.multiple_of(r * row_chunk, row_chunk), row_chunk)
            h_ref[rows, :] = _rms_rows(x_ref[rows, :], gain_ref[...]).astype(BF16)
            return carry

        lax.fori_loop(0, tm // row_chunk, body, 0)

    z_ref[...] = jnp.dot(h_ref[...], w_ref[...], preferred_element_type=F32)


def _in_proj(x, gain, w, *, tm=1024, tn=512):
    s, d = x.shape
    n = w.shape[1]
    return pl.pallas_call(
        functools.partial(_in_proj_kernel, row_chunk=128),
        name="in_proj",
        grid=(s // tm, n // tn),
        in_specs=[
            pl.BlockSpec((tm, d), lambda i, j: (i, 0)),
            pl.BlockSpec((1, d), lambda i, j: (0, 0)),
            pl.BlockSpec((d, tn), lambda i, j: (0, j)),
        ],
        out_specs=pl.BlockSpec((tm, tn), lambda i, j: (i, j)),
        out_shape=jax.ShapeDtypeStruct((s, n), F32),
        scratch_shapes=[pltpu.VMEM((tm, d), BF16)],
        compiler_params=pltpu.CompilerParams(
            dimension_semantics=("parallel", "arbitrary"),
            vmem_limit_bytes=VMEM_LIMIT_BYTES,
        ),
    )(x, gain, w)


def _dot_nt(a, b):
    return lax.dot_general(a, b, (((1,), (1,)), ((), ())), preferred_element_type=F32)


def _dot_tn(a, b):
    return lax.dot_general(a, b, (((0,), (0,)), ((), ())), preferred_element_type=F32)


def _mixer_kernel(sink_ref, cdecay_ref, z_ref, cos_r_ref, sin_r_ref, cos_a_ref, sin_lo_ref,
                  sin_hi_ref, qn_ref, kn_ref, decay_ref, qhead_ref, ktail_ref,
                  y_ref, state_ref, kprev_ref, vprev_ref):
    c = pl.program_id(0)

    @pl.when(c == 0)
    def _():
        state_ref[...] = jnp.zeros_like(state_ref)
        kprev_ref[...] = jnp.zeros_like(kprev_ref)
        vprev_ref[...] = jnp.zeros_like(vprev_ref)

    def head(off, h):
        return z_ref[:, off + h * HEAD_DIM: off + (h + 1) * HEAD_DIM]

    cos_r = cos_r_ref[...]
    sin_r = sin_r_ref[...]

    def rot_r(t):
        return t * cos_r + pltpu.roll(t, HEAD_DIM // 2, 1) * sin_r

    for h in range(RET_HEADS):
        q = rot_r(head(OFF_QR, h))
        k = rot_r(head(OFF_KR, h)) * (HEAD_DIM ** -0.5)
        vb = head(OFF_VR, h).astype(BF16)
        scores = _dot_nt(q.astype(BF16), k.astype(BF16)) * decay_ref[h]
        intra = jnp.dot(scores.astype(BF16), vb, preferred_element_type=F32)
        chunk_kv = _dot_tn((k * ktail_ref[h]).astype(BF16), vb)
        prev = state_ref[h]
        cross = jnp.dot((q * qhead_ref[h]).astype(BF16), prev.astype(BF16),
                        preferred_element_type=F32)
        state_ref[h] = prev * cdecay_ref[h] + chunk_kv
        y = intra + cross
        y = y * lax.rsqrt(jnp.mean(y * y, axis=-1, keepdims=True) + EPS)
        g = head(OFF_GR, h)
        y_ref[:, h * HEAD_DIM:(h + 1) * HEAD_DIM] = ((g * jax.nn.sigmoid(g)) * y).astype(BF16)

    cos_a = cos_a_ref[...]
    sin_lo = sin_lo_ref[...]
    sin_hi = sin_hi_ref[...]
    half = ROPE_DIM // 2

    def rot_a(t):
        return (t * cos_a + pltpu.roll(t, HEAD_DIM - half, 1) * sin_lo
                + pltpu.roll(t, half, 1) * sin_hi)

    qi = lax.broadcasted_iota(jnp.int32, (WINDOW, 2 * WINDOW), 0)
    kj = lax.broadcasted_iota(jnp.int32, (WINDOW, 2 * WINDOW), 1)
    rel = WINDOW + qi - kj
    mask = (rel >= 0) & (rel < WINDOW) & ((kj >= WINDOW) | (c > 0))
    neg = jnp.finfo(F32).min

    for kh in range(ATT_KV_HEADS):
        kb = rot_a(_rms_rows(head(OFF_KA, kh), kn_ref[...])).astype(BF16)
        vb = head(OFF_VA, kh).astype(BF16)
        kk = jnp.concatenate([kprev_ref[kh], kb], axis=0)
        vv = jnp.concatenate([vprev_ref[kh], vb], axis=0)
        for gq in range(GQA_GROUP):
            qh = kh * GQA_GROUP + gq
            qb = rot_a(_rms_rows(head(OFF_QA, qh), qn_ref[...])).astype(BF16)
            s = _dot_nt(qb, kk) * (HEAD_DIM ** -0.5)
            s = jnp.where(mask, s, neg)
            sink = sink_ref[qh]
            m = jnp.maximum(jnp.max(s, axis=-1, keepdims=True), sink)
            p = jnp.exp(s - m)
            denom = jnp.sum(p, axis=-1, keepdims=True) + jnp.exp(sink - m)
            p = p / denom
            o = jnp.dot(p.astype(BF16), vv, preferred_element_type=F32)
            col = RET_WIDTH + qh * HEAD_DIM
            y_ref[:, col:col + HEAD_DIM] = o.astype(BF16)
        kprev_ref[kh] = kb
        vprev_ref[kh] = vb


def _mixer(z, sinks, tables, q_gain, k_gain):
    s = z.shape[0]
    nc = s // CHUNK
    row_tab = pl.BlockSpec((CHUNK, HEAD_DIM), lambda c: (c, 0))
    whole = lambda shape: pl.BlockSpec(shape, lambda c: (0,) * len(shape))
    smem = pl.BlockSpec(memory_space=pltpu.SMEM)
    head_tab = (RET_HEADS, CHUNK, HEAD_DIM)
    return pl.pallas_call(
        _mixer_kernel,
        name="mixer",
        grid=(nc,),
        in_specs=[
            smem, smem,
            pl.BlockSpec((CHUNK, IN_WIDTH), lambda c: (c, 0)),
            row_tab, row_tab, row_tab, row_tab, row_tab,
            whole((1, HEAD_DIM)), whole((1, HEAD_DIM)),
            whole(head_tab), whole(head_tab), whole(head_tab),
        ],
        out_specs=pl.BlockSpec((CHUNK, MIX_WIDTH), lambda c: (c, 0)),
        out_shape=jax.ShapeDtypeStruct((s, MIX_WIDTH), BF16),
        scratch_shapes=[
            pltpu.VMEM((RET_HEADS, HEAD_DIM, HEAD_DIM), F32),
            pltpu.VMEM((ATT_KV_HEADS, WINDOW, HEAD_DIM), BF16),
            pltpu.VMEM((ATT_KV_HEADS, WINDOW, HEAD_DIM), BF16),
        ],
        compiler_params=pltpu.CompilerParams(
            dimension_semantics=("arbitrary",),
            vmem_limit_bytes=VMEM_LIMIT_BYTES,
        ),
    )(sinks, tables["chunk_decay"], z, tables["cos_r"], tables["sin_r"], tables["cos_a"],
      tables["sin_lo"], tables["sin_hi"], q_gain, k_gain, tables["decay"], tables["q_head"],
      tables["k_tail"])


def _out_proj_kernel(x_ref, y_ref, w_ref, o_ref):
    o_ref[...] = x_ref[...] + jnp.dot(y_ref[...], w_ref[...], preferred_element_type=F32)


def _out_proj(x, y, w, *, tm=512):
    s, d = x.shape
    k = y.shape[1]
    return pl.pallas_call(
        _out_proj_kernel,
        name="out_proj",
        grid=(s // tm,),
        in_specs=[
            pl.BlockSpec((tm, d), lambda i: (i, 0)),
            pl.BlockSpec((tm, k), lambda i: (i, 0)),
            pl.BlockSpec((k, d), lambda i: (0, 0)),
        ],
        out_specs=pl.BlockSpec((tm, d), lambda i: (i, 0)),
        out_shape=jax.ShapeDtypeStruct((s, d), F32),
        compiler_params=pltpu.CompilerParams(
            dimension_semantics=("parallel",),
            vmem_limit_bytes=VMEM_LIMIT_BYTES,
        ),
    )(x, y, w)


def _rope_tables(seq_len, dim, theta):
    inv = theta ** (-jnp.arange(0, dim, 2, dtype=F32) / dim)
    ang = jnp.arange(seq_len, dtype=F32)[:, None] * inv[None, :]
    ang = jnp.concatenate([ang, ang], axis=-1)
    return jnp.cos(ang), jnp.sin(ang)


def _position_tables(seq_len):
    cos_r, sin_r = _rope_tables(seq_len, HEAD_DIM, RET_THETA)
    hd2 = HEAD_DIM // 2
    sin_r = jnp.concatenate([-sin_r[:, :hd2], sin_r[:, hd2:]], axis=-1)

    cos_a, sin_a = _rope_tables(seq_len, ROPE_DIM, ROPE_THETA)
    half = ROPE_DIM // 2
    pad = lambda t, lo, fill: jnp.concatenate(
        [jnp.zeros((seq_len, lo), F32), t,
         jnp.full((seq_len, HEAD_DIM - lo - t.shape[1]), fill, F32)], axis=-1)
    cos_full = pad(cos_a, 0, 1.0)
    sin_lo = pad(-sin_a[:, :half], 0, 0.0)
    sin_hi = pad(sin_a[:, half:], half, 0.0)

    log_g = jnp.log1p(-jnp.exp2(-5.0 - jnp.arange(RET_HEADS, dtype=F32)))
    pos = jnp.arange(CHUNK, dtype=F32)
    diff = pos[:, None] - pos[None, :]
    decay = jnp.where(diff[None] >= 0,
                      jnp.exp(jnp.maximum(diff, 0.0)[None] * log_g[:, None, None]), 0.0)
    k_tail = jnp.exp((CHUNK - 1.0 - pos)[None, :] * log_g[:, None])
    q_head = jnp.exp((pos + 1.0)[None, :] * log_g[:, None])
    bcast = lambda t: jnp.broadcast_to(t[:, :, None], (RET_HEADS, CHUNK, HEAD_DIM))
    return {
        "cos_r": cos_r, "sin_r": sin_r, "cos_a": cos_full, "sin_lo": sin_lo, "sin_hi": sin_hi,
        "decay": decay, "k_tail": bcast(k_tail), "q_head": bcast(q_head),
        "chunk_decay": jnp.exp(CHUNK * log_g),
    }


def kernel(x, ffn1_norm, ffn1_w_gate, ffn1_w_up, ffn1_w_down, mix_norm, w_in, q_norm, k_norm,
           attn_sinks, w_out, ffn2_norm, ffn2_w_gate, ffn2_w_up, ffn2_w_down):
    b, s, d = x.shape
    depth = ffn1_norm.shape[0]
    tables = _position_tables(s)
    outs = []
    for bi in range(b):
        xb = x[bi]
        for l in range(depth):
            xb = _ffn(xb, ffn1_norm[l][None], ffn1_w_gate[l], ffn1_w_up[l], ffn1_w_down[l])
            z = _in_proj(xb, mix_norm[l][None], w_in[l].astype(BF16))
            y = _mixer(z, attn_sinks[l], tables, q_norm[l][None], k_norm[l][None])
            xb = _out_proj(xb, y, w_out[l].astype(BF16))
            xb = _ffn(xb, ffn2_norm[l][None], ffn2_w_gate[l], ffn2_w_up[l], ffn2_w_down[l])
        outs.append(xb)
    return jnp.stack(outs, axis=0)
```

```python
import functools

import jax
import jax.numpy as jnp
from jax import lax
from jax.experimental import pallas as pl
from jax.experimental.pallas import tpu as pltpu

D_MODEL = 2048
HEAD_DIM = 128
RET_HEADS = 8
ATT_Q_HEADS = 8
ATT_KV_HEADS = 2
GQA_GROUP = ATT_Q_HEADS // ATT_KV_HEADS
RET_WIDTH = RET_HEADS * HEAD_DIM
ATT_WIDTH = ATT_Q_HEADS * HEAD_DIM
KV_WIDTH = ATT_KV_HEADS * HEAD_DIM
MIX_WIDTH = RET_WIDTH + ATT_WIDTH
IN_WIDTH = 4 * RET_WIDTH + ATT_WIDTH + 2 * KV_WIDTH
CHUNK = 128
WINDOW = 128
ROPE_THETA = 500000.0
ROPE_DIM = HEAD_DIM // 4
RET_THETA = 10000.0
EPS = 1e-6

OFF_QR = 0
OFF_KR = RET_WIDTH
OFF_VR = 2 * RET_WIDTH
OFF_GR = 3 * RET_WIDTH
OFF_QA = 4 * RET_WIDTH
OFF_KA = OFF_QA + ATT_WIDTH
OFF_VA = OFF_KA + KV_WIDTH

VMEM_LIMIT_BYTES = 60 * 1024 * 1024

F32 = jnp.float32
BF16 = jnp.bfloat16


def _rms_rows(x, gain):
    ms = jnp.mean(x * x, axis=-1, keepdims=True)
    return x * lax.rsqrt(ms + EPS) * gain


def _ffn_kernel(x_hbm, gain_ref, wg_ref, wu_ref, wd_ref, o_ref, h_ref, sem, *, row_chunk):
    i = pl.program_id(0)
    j = pl.program_id(1)
    tm = o_ref.shape[0]
    n_chunks = tm // row_chunk

    @pl.when(j == 0)
    def _():
        def x_copy(r):
            return pltpu.make_async_copy(
                x_hbm.at[pl.ds(i * tm + r * row_chunk, row_chunk), :],
                o_ref.at[pl.ds(r * row_chunk, row_chunk), :],
                sem.at[r])

        for r in range(n_chunks):
            x_copy(r).start()
        for r in range(n_chunks):
            x_copy(r).wait()
            rows = pl.ds(r * row_chunk, row_chunk)
            h_ref[rows, :] = _rms_rows(o_ref[rows, :], gain_ref[...]).astype(BF16)

    h = h_ref[...]
    g = jnp.dot(h, wg_ref[...].astype(BF16), preferred_element_type=F32)
    u = jnp.dot(h, wu_ref[...].astype(BF16), preferred_element_type=F32)
    a = (0.5 * (g * jax.nn.sigmoid(g)) * u).astype(BF16)
    o_ref[...] += jnp.dot(a, wd_ref[...].astype(BF16), preferred_element_type=F32)


def _ffn(x, gain, wg, wu, wd, *, tm=1024, tf=512, row_chunk=128):
    s, d = x.shape
    d_ff = wg.shape[1]
    return pl.pallas_call(
        functools.partial(_ffn_kernel, row_chunk=row_chunk),
        name="ffn",
        grid=(s // tm, d_ff // tf),
        in_specs=[
            pl.BlockSpec(memory_space=pl.ANY),
            pl.BlockSpec((1, d), lambda i, j: (0, 0)),
            pl.BlockSpec((d, tf), lambda i, j: (0, j)),
            pl.BlockSpec((d, tf), lambda i, j: (0, j)),
            pl.BlockSpec((tf, d), lambda i, j: (j, 0)),
        ],
        out_specs=pl.BlockSpec((tm, d), lambda i, j: (i, 0)),
        out_shape=jax.ShapeDtypeStruct((s, d), F32),
        scratch_shapes=[pltpu.VMEM((tm, d), BF16), pltpu.SemaphoreType.DMA((tm // row_chunk,))],
        compiler_params=pltpu.CompilerParams(
            dimension_semantics=("parallel", "arbitrary"),
            vmem_limit_bytes=VMEM_LIMIT_BYTES,
        ),
    )(x, gain, wg, wu, wd)


def _dot_nt(a, b):
    return lax.dot_general(a, b, (((1,), (1,)), ((), ())), preferred_element_type=F32)


def _dot_tn(a, b):
    return lax.dot_general(a, b, (((0,), (0,)), ((), ())), preferred_element_type=F32)


def _mix_chunk(zc, rows, not_first, sink_ref, cdecay_ref, tabs, qn_ref, kn_ref, decay_ref,
               qhead_ref, ktail_ref, y_ref, state_ref, kprev_ref, vprev_ref):
    cos_r_ref, sin_r_ref, cos_a_ref, sin_lo_ref, sin_hi_ref = tabs

    cos_r = cos_r_ref[rows, :]
    sin_r = sin_r_ref[rows, :]

    def rot_r(t):
        return t * cos_r + pltpu.roll(t, HEAD_DIM // 2, 1) * sin_r

    for h in range(RET_HEADS):
        q = rot_r(zc(OFF_QR, h))
        k = rot_r(zc(OFF_KR, h)) * (HEAD_DIM ** -0.5)
        vb = zc(OFF_VR, h).astype(BF16)
        scores = _dot_nt(q.astype(BF16), k.astype(BF16)) * decay_ref[h]
        intra = jnp.dot(scores.astype(BF16), vb, preferred_element_type=F32)
        chunk_kv = _dot_tn((k * ktail_ref[h]).astype(BF16), vb)
        prev = state_ref[h]
        cross = jnp.dot((q * qhead_ref[h]).astype(BF16), prev.astype(BF16),
                        preferred_element_type=F32)
        state_ref[h] = prev * cdecay_ref[h] + chunk_kv
        y = intra + cross
        y = y * lax.rsqrt(jnp.mean(y * y, axis=-1, keepdims=True) + EPS)
        g = zc(OFF_GR, h)
        y_ref[rows, h * HEAD_DIM:(h + 1) * HEAD_DIM] = ((g * jax.nn.sigmoid(g)) * y).astype(BF16)

    cos_a = cos_a_ref[rows, :]
    sin_lo = sin_lo_ref[rows, :]
    sin_hi = sin_hi_ref[rows, :]
    half = ROPE_DIM // 2

    def rot_a(t):
        return (t * cos_a + pltpu.roll(t, HEAD_DIM - half, 1) * sin_lo
                + pltpu.roll(t, half, 1) * sin_hi)

    qi = lax.broadcasted_iota(jnp.int32, (WINDOW, 2 * WINDOW), 0)
    kj = lax.broadcasted_iota(jnp.int32, (WINDOW, 2 * WINDOW), 1)
    rel = WINDOW + qi - kj
    mask = (rel >= 0) & (rel < WINDOW) & ((kj >= WINDOW) | not_first)
    neg = jnp.finfo(F32).min

    for kh in range(ATT_KV_HEADS):
        kb = rot_a(_rms_rows(zc(OFF_KA, kh), kn_ref[...])).astype(BF16)
        vb = zc(OFF_VA, kh).astype(BF16)
        kk = jnp.concatenate([kprev_ref[kh], kb], axis=0)
        vv = jnp.concatenate([vprev_ref[kh], vb], axis=0)
        for gq in range(GQA_GROUP):
            qh = kh * GQA_GROUP + gq
            qb = rot_a(_rms_rows(zc(OFF_QA, qh), qn_ref[...])).astype(BF16)
            s = _dot_nt(qb, kk) * (HEAD_DIM ** -0.5)
            s = jnp.where(mask, s, neg)
            sink = sink_ref[qh]
            m = jnp.maximum(jnp.max(s, axis=-1, keepdims=True), sink)
            p = jnp.exp(s - m)
            denom = jnp.sum(p, axis=-1, keepdims=True) + jnp.exp(sink - m)
            p = p / denom
            o = jnp.dot(p.astype(BF16), vv, preferred_element_type=F32)
            col = RET_WIDTH + qh * HEAD_DIM
            y_ref[rows, col:col + HEAD_DIM] = o.astype(BF16)
        kprev_ref[kh] = kb
        vprev_ref[kh] = vb


def _mix_layer_kernel(sink_ref, cdecay_ref, xn_ref, gain_ref, win_ref, cos_r_ref,
                      sin_r_ref, cos_a_ref, sin_lo_ref, sin_hi_ref, qn_ref, kn_ref, decay_ref,
                      qhead_ref, ktail_ref, wout_ref, o_ref,
                      z_ref, xprev_ref, y_ref, state_ref, kprev_ref, vprev_ref):
    s = pl.program_id(0)
    rb = xn_ref.shape[0]

    @pl.when(s == 0)
    def _():
        z_ref[1] = jnp.zeros(z_ref.shape[1:], F32)
        xprev_ref[...] = jnp.zeros_like(xprev_ref)

    @pl.when(s <= 1)
    def _():
        state_ref[...] = jnp.zeros_like(state_ref)
        kprev_ref[...] = jnp.zeros_like(kprev_ref)
        vprev_ref[...] = jnp.zeros_like(vprev_ref)

    w_slot = s % 2
    r_slot = 1 - w_slot

    h = _rms_rows(xn_ref[...], gain_ref[...]).astype(BF16)
    z_ref[w_slot] = jnp.dot(h, win_ref[...], preferred_element_type=F32)

    tabs = (cos_r_ref, sin_r_ref, cos_a_ref, sin_lo_ref, sin_hi_ref)
    for cc in range(rb // CHUNK):
        rows = slice(cc * CHUNK, (cc + 1) * CHUNK)

        def zc(off, hd, rows=rows):
            return z_ref[r_slot, rows, off + hd * HEAD_DIM: off + (hd + 1) * HEAD_DIM]

        not_first = (s > 1) if cc == 0 else True
        _mix_chunk(zc, rows, not_first, sink_ref, cdecay_ref, tabs, qn_ref, kn_ref, decay_ref,
                   qhead_ref, ktail_ref, y_ref, state_ref, kprev_ref, vprev_ref)

    o_ref[...] = xprev_ref[...] + jnp.dot(y_ref[...], wout_ref[...],
                                          preferred_element_type=F32)
    xprev_ref[...] = xn_ref[...]


def _mix_layer(x, gain, w_in, w_out, sinks, tables, q_gain, k_gain, *, rb=256):
    s, d = x.shape
    nb = s // rb
    cur = lambda t: (jnp.minimum(t, nb - 1), 0)
    prv = lambda t: (jnp.maximum(t - 1, 0), 0)
    const2 = lambda t: (0, 0)
    const3 = lambda t: (0, 0, 0)
    resident = dict(pipeline_mode=pl.Buffered(1))
    row_tab = pl.BlockSpec((rb, HEAD_DIM), prv)
    smem = pl.BlockSpec(memory_space=pltpu.SMEM)
    head_tab = pl.BlockSpec((RET_HEADS, CHUNK, HEAD_DIM), const3, **resident)
    return pl.pallas_call(
        _mix_layer_kernel,
        name="mix_layer",
        grid=(nb + 1,),
        in_specs=[
            smem, smem,
            pl.BlockSpec((rb, d), cur),
            pl.BlockSpec((1, d), const2),
            pl.BlockSpec((d, IN_WIDTH), const2, **resident),
            row_tab, row_tab, row_tab, row_tab, row_tab,
            pl.BlockSpec((1, HEAD_DIM), const2), pl.BlockSpec((1, HEAD_DIM), const2),
            head_tab, head_tab, head_tab,
            pl.BlockSpec((MIX_WIDTH, d), const2, **resident),
        ],
        out_specs=pl.BlockSpec((rb, d), prv),
        out_shape=jax.ShapeDtypeStruct((s, d), F32),
        scratch_shapes=[
            pltpu.VMEM((2, rb, IN_WIDTH), F32),
            pltpu.VMEM((rb, d), F32),
            pltpu.VMEM((rb, MIX_WIDTH), BF16),
            pltpu.VMEM((RET_HEADS, HEAD_DIM, HEAD_DIM), F32),
            pltpu.VMEM((ATT_KV_HEADS, WINDOW, HEAD_DIM), BF16),
            pltpu.VMEM((ATT_KV_HEADS, WINDOW, HEAD_DIM), BF16),
        ],
        compiler_params=pltpu.CompilerParams(
            dimension_semantics=("arbitrary",),
            vmem_limit_bytes=VMEM_LIMIT_BYTES,
        ),
    )(sinks, tables["chunk_decay"], x, gain, w_in, tables["cos_r"], tables["sin_r"],
      tables["cos_a"], tables["sin_lo"], tables["sin_hi"], q_gain, k_gain, tables["decay"],
      tables["q_head"], tables["k_tail"], w_out)


def _rope_tables(seq_len, dim, theta):
    inv = theta ** (-jnp.arange(0, dim, 2, dtype=F32) / dim)
    ang = jnp.arange(seq_len, dtype=F32)[:, None] * inv[None, :]
    ang = jnp.concatenate([ang, ang], axis=-1)
    return jnp.cos(ang), jnp.sin(ang)


def _position_tables(seq_len):
    cos_r, sin_r = _rope_tables(seq_len, HEAD_DIM, RET_THETA)
    hd2 = HEAD_DIM // 2
    sin_r = jnp.concatenate([-sin_r[:, :hd2], sin_r[:, hd2:]], axis=-1)

    cos_a, sin_a = _rope_tables(seq_len, ROPE_DIM, ROPE_THETA)
    half = ROPE_DIM // 2
    pad = lambda t, lo, fill: jnp.concatenate(
        [jnp.zeros((seq_len, lo), F32), t,
         jnp.full((seq_len, HEAD_DIM - lo - t.shape[1]), fill, F32)], axis=-1)
    cos_full = pad(cos_a, 0, 1.0)
    sin_lo = pad(-sin_a[:, :half], 0, 0.0)
    sin_hi = pad(sin_a[:, half:], half, 0.0)

    log_g = jnp.log1p(-jnp.exp2(-5.0 - jnp.arange(RET_HEADS, dtype=F32)))
    pos = jnp.arange(CHUNK, dtype=F32)
    diff = pos[:, None] - pos[None, :]
    decay = jnp.where(diff[None] >= 0,
                      jnp.exp(jnp.maximum(diff, 0.0)[None] * log_g[:, None, None]), 0.0)
    k_tail = jnp.exp((CHUNK - 1.0 - pos)[None, :] * log_g[:, None])
    q_head = jnp.exp((pos + 1.0)[None, :] * log_g[:, None])
    bcast = lambda t: jnp.broadcast_to(t[:, :, None], (RET_HEADS, CHUNK, HEAD_DIM))
    return {
        "cos_r": cos_r, "sin_r": sin_r, "cos_a": cos_full, "sin_lo": sin_lo, "sin_hi": sin_hi,
        "decay": decay, "k_tail": bcast(k_tail), "q_head": bcast(q_head),
        "chunk_decay": jnp.exp(CHUNK * log_g),
    }


def kernel(x, ffn1_norm, ffn1_w_gate, ffn1_w_up, ffn1_w_down, mix_norm, w_in, q_norm, k_norm,
           attn_sinks, w_out, ffn2_norm, ffn2_w_gate, ffn2_w_up, ffn2_w_down):
    b, s, d = x.shape
    depth = ffn1_norm.shape[0]
    tables = _position_tables(s)
    outs = []
    for bi in range(b):
        xb = x[bi]
        for l in range(depth):
            xb = _ffn(xb, ffn1_norm[l][None], ffn1_w_gate[l], ffn1_w_up[l], ffn1_w_down[l])
            xb = _mix_layer(xb, mix_norm[l][None], w_in[l].astype(BF16), w_out[l].astype(BF16),
                            attn_sinks[l], tables, q_norm[l][None], k_norm[l][None])
            xb = _ffn(xb, ffn2_norm[l][None], ffn2_w_gate[l], ffn2_w_up[l], ffn2_w_down[l])
        outs.append(xb)
    return jnp.stack(outs, axis=0)
```

```python
import functools

import jax
import jax.numpy as jnp
from jax import lax
from jax.experimental import pallas as pl
from jax.experimental.pallas import tpu as pltpu

D_MODEL = 2048
HEAD_DIM = 128
RET_HEADS = 8
ATT_Q_HEADS = 8
ATT_KV_HEADS = 2
GQA_GROUP = ATT_Q_HEADS // ATT_KV_HEADS
RET_WIDTH = RET_HEADS * HEAD_DIM
ATT_WIDTH = ATT_Q_HEADS * HEAD_DIM
KV_WIDTH = ATT_KV_HEADS * HEAD_DIM
MIX_WIDTH = RET_WIDTH + ATT_WIDTH
IN_WIDTH = 4 * RET_WIDTH + ATT_WIDTH + 2 * KV_WIDTH
CHUNK = 128
WINDOW = 128
ROPE_THETA = 500000.0
ROPE_DIM = HEAD_DIM // 4
RET_THETA = 10000.0
EPS = 1e-6

OFF_QR = 0
OFF_KR = RET_WIDTH
OFF_VR = 2 * RET_WIDTH
OFF_GR = 3 * RET_WIDTH
OFF_QA = 4 * RET_WIDTH
OFF_KA = OFF_QA + ATT_WIDTH
OFF_VA = OFF_KA + KV_WIDTH

VMEM_LIMIT_BYTES = 60 * 1024 * 1024

F32 = jnp.float32
BF16 = jnp.bfloat16


def _rms_rows(x, gain):
    ms = jnp.mean(x * x, axis=-1, keepdims=True)
    return x * lax.rsqrt(ms + EPS) * gain


def _ffn_kernel(x_hbm, gain_ref, wg0_ref, wg1_ref, wu0_ref, wu1_ref, wd0_ref, wd1_ref,
                o_ref, h_ref, sem, *, row_chunk):
    i = pl.program_id(0)
    j = pl.program_id(1)
    tm = o_ref.shape[0]
    n_chunks = tm // row_chunk

    @pl.when(j == 0)
    def _():
        def x_copy(r):
            return pltpu.make_async_copy(
                x_hbm.at[pl.ds(i * tm + r * row_chunk, row_chunk), :],
                o_ref.at[pl.ds(r * row_chunk, row_chunk), :],
                sem.at[r])

        for r in range(n_chunks):
            x_copy(r).start()
        for r in range(n_chunks):
            x_copy(r).wait()
            rows = pl.ds(r * row_chunk, row_chunk)
            h_ref[rows, :] = _rms_rows(o_ref[rows, :], gain_ref[...]).astype(BF16)

    h = h_ref[...]
    acc = None
    for wg_ref, wu_ref, wd_ref in ((wg0_ref, wu0_ref, wd0_ref), (wg1_ref, wu1_ref, wd1_ref)):
        g = jnp.dot(h, wg_ref[...].astype(BF16), preferred_element_type=F32)
        u = jnp.dot(h, wu_ref[...].astype(BF16), preferred_element_type=F32)
        a = (0.5 * (g * jax.nn.sigmoid(g)) * u).astype(BF16)
        part = jnp.dot(a, wd_ref[...].astype(BF16), preferred_element_type=F32)
        acc = part if acc is None else acc + part
    o_ref[...] += acc


def _ffn(x, gain, wg, wu, wd, *, tm=1024, tf=512, row_chunk=128):
    s, d = x.shape
    d_ff = wg.shape[1]
    th = tf // 2
    col = lambda half: pl.BlockSpec((d, th), lambda i, j: (0, 2 * j + half))
    row = lambda half: pl.BlockSpec((th, d), lambda i, j: (2 * j + half, 0))
    return pl.pallas_call(
        functools.partial(_ffn_kernel, row_chunk=row_chunk),
        name="ffn",
        grid=(s // tm, d_ff // tf),
        in_specs=[
            pl.BlockSpec(memory_space=pl.ANY),
            pl.BlockSpec((1, d), lambda i, j: (0, 0)),
            col(0), col(1), col(0), col(1), row(0), row(1),
        ],
        out_specs=pl.BlockSpec((tm, d), lambda i, j: (i, 0)),
        out_shape=jax.ShapeDtypeStruct((s, d), F32),
        scratch_shapes=[pltpu.VMEM((tm, d), BF16), pltpu.SemaphoreType.DMA((tm // row_chunk,))],
        compiler_params=pltpu.CompilerParams(
            dimension_semantics=("parallel", "arbitrary"),
            vmem_limit_bytes=VMEM_LIMIT_BYTES,
        ),
    )(x, gain, wg, wg, wu, wu, wd, wd)


def _dot_nt(a, b):
    return lax.dot_general(a, b, (((1,), (1,)), ((), ())), preferred_element_type=F32)


def _dot_tn(a, b):
    return lax.dot_general(a, b, (((0,), (0,)), ((), ())), preferred_element_type=F32)


def _mix_chunk(zc, rows, not_first, sink_ref, cdecay_ref, tabs, qn_ref, kn_ref, decay_ref,
               qhead_ref, ktail_ref, y_ref, state_ref, kprev_ref, vprev_ref):
    cos_r_ref, sin_r_ref, cos_a_ref, sin_lo_ref, sin_hi_ref = tabs

    cos_r = cos_r_ref[rows, :]
    sin_r = sin_r_ref[rows, :]

    def rot_r(t):
        return t * cos_r + pltpu.roll(t, HEAD_DIM // 2, 1) * sin_r

    for h in range(RET_HEADS):
        q = rot_r(zc(OFF_QR, h))
        k = rot_r(zc(OFF_KR, h)) * (HEAD_DIM ** -0.5)
        vb = zc(OFF_VR, h).astype(BF16)
        scores = _dot_nt(q.astype(BF16), k.astype(BF16)) * decay_ref[h]
        intra = jnp.dot(scores.astype(BF16), vb, preferred_element_type=F32)
        chunk_kv = _dot_tn((k * ktail_ref[h]).astype(BF16), vb)
        prev = state_ref[h]
        cross = jnp.dot((q * qhead_ref[h]).astype(BF16), prev.astype(BF16),
                        preferred_element_type=F32)
        state_ref[h] = prev * cdecay_ref[h] + chunk_kv
        y = intra + cross
        y = y * lax.rsqrt(jnp.mean(y * y, axis=-1, keepdims=True) + EPS)
        g = zc(OFF_GR, h)
        y_ref[rows, h * HEAD_DIM:(h + 1) * HEAD_DIM] = ((g * jax.nn.sigmoid(g)) * y).astype(BF16)

    cos_a = cos_a_ref[rows, :]
    sin_lo = sin_lo_ref[rows, :]
    sin_hi = sin_hi_ref[rows, :]
    half = ROPE_DIM // 2

    def rot_a(t):
        return (t * cos_a + pltpu.roll(t, HEAD_DIM - half, 1) * sin_lo
                + pltpu.roll(t, half, 1) * sin_hi)

    qi = lax.broadcasted_iota(jnp.int32, (WINDOW, 2 * WINDOW), 0)
    kj = lax.broadcasted_iota(jnp.int32, (WINDOW, 2 * WINDOW), 1)
    rel = WINDOW + qi - kj
    mask = (rel >= 0) & (rel < WINDOW) & ((kj >= WINDOW) | not_first)
    neg = jnp.finfo(F32).min

    for kh in range(ATT_KV_HEADS):
        kb = rot_a(_rms_rows(zc(OFF_KA, kh), kn_ref[...])).astype(BF16)
        vb = zc(OFF_VA, kh).astype(BF16)
        kk = jnp.concatenate([kprev_ref[kh], kb], axis=0)
        vv = jnp.concatenate([vprev_ref[kh], vb], axis=0)
        for gq in range(GQA_GROUP):
            qh = kh * GQA_GROUP + gq
            qb = rot_a(_rms_rows(zc(OFF_QA, qh), qn_ref[...])).astype(BF16)
            s = _dot_nt(qb, kk) * (HEAD_DIM ** -0.5)
            s = jnp.where(mask, s, neg)
            sink = sink_ref[qh]
            m = jnp.maximum(jnp.max(s, axis=-1, keepdims=True), sink)
            p = jnp.exp(s - m)
            denom = jnp.sum(p, axis=-1, keepdims=True) + jnp.exp(sink - m)
            p = p / denom
            o = jnp.dot(p.astype(BF16), vv, preferred_element_type=F32)
            col = RET_WIDTH + qh * HEAD_DIM
            y_ref[rows, col:col + HEAD_DIM] = o.astype(BF16)
        kprev_ref[kh] = kb
        vprev_ref[kh] = vb


def _mix_layer_kernel(sink_ref, cdecay_ref, xn_ref, gain_ref, win_ref, cos_r_ref,
                      sin_r_ref, cos_a_ref, sin_lo_ref, sin_hi_ref, qn_ref, kn_ref, decay_ref,
                      qhead_ref, ktail_ref, wout_ref, o_ref,
                      z_ref, xprev_ref, y_ref, state_ref, kprev_ref, vprev_ref):
    s = pl.program_id(0)
    rb = xn_ref.shape[0]

    @pl.when(s == 0)
    def _():
        z_ref[1] = jnp.zeros(z_ref.shape[1:], F32)
        xprev_ref[...] = jnp.zeros_like(xprev_ref)

    @pl.when(s <= 1)
    def _():
        state_ref[...] = jnp.zeros_like(state_ref)
        kprev_ref[...] = jnp.zeros_like(kprev_ref)
        vprev_ref[...] = jnp.zeros_like(vprev_ref)

    w_slot = s % 2
    r_slot = 1 - w_slot

    h = _rms_rows(xn_ref[...], gain_ref[...]).astype(BF16)
    z_ref[w_slot] = jnp.dot(h, win_ref[...], preferred_element_type=F32)

    tabs = (cos_r_ref, sin_r_ref, cos_a_ref, sin_lo_ref, sin_hi_ref)
    for cc in range(rb // CHUNK):
        rows = slice(cc * CHUNK, (cc + 1) * CHUNK)

        def zc(off, hd, rows=rows):
            return z_ref[r_slot, rows, off + hd * HEAD_DIM: off + (hd + 1) * HEAD_DIM]

        not_first = (s > 1) if cc == 0 else True
        _mix_chunk(zc, rows, not_first, sink_ref, cdecay_ref, tabs, qn_ref, kn_ref, decay_ref,
                   qhead_ref, ktail_ref, y_ref, state_ref, kprev_ref, vprev_ref)

    o_ref[...] = xprev_ref[...] + jnp.dot(y_ref[...], wout_ref[...],
                                          preferred_element_type=F32)
    xprev_ref[...] = xn_ref[...]


def _mix_layer(x, gain, w_in, w_out, sinks, tables, q_gain, k_gain, *, rb=256):
    s, d = x.shape
    nb = s // rb
    cur = lambda t: (jnp.minimum(t, nb - 1), 0)
    prv = lambda t: (jnp.maximum(t - 1, 0), 0)
    const2 = lambda t: (0, 0)
    const3 = lambda t: (0, 0, 0)
    resident = dict(pipeline_mode=pl.Buffered(1))
    row_tab = pl.BlockSpec((rb, HEAD_DIM), prv)
    smem = pl.BlockSpec(memory_space=pltpu.SMEM)
    head_tab = pl.BlockSpec((RET_HEADS, CHUNK, HEAD_DIM), const3, **resident)
    return pl.pallas_call(
        _mix_layer_kernel,
        name="mix_layer",
        grid=(nb + 1,),
        in_specs=[
            smem, smem,
            pl.BlockSpec((rb, d), cur),
            pl.BlockSpec((1, d), const2),
            pl.BlockSpec((d, IN_WIDTH), const2, **resident),
            row_tab, row_tab, row_tab, row_tab, row_tab,
            pl.BlockSpec((1, HEAD_DIM), const2), pl.BlockSpec((1, HEAD_DIM), const2),
            head_tab, head_tab, head_tab,
            pl.BlockSpec((MIX_WIDTH, d), const2, **resident),
        ],
        out_specs=pl.BlockSpec((rb, d), prv),
        out_shape=jax.ShapeDtypeStruct((s, d), F32),
        scratch_shapes=[
            pltpu.VMEM((2, rb, IN_WIDTH), F32),
            pltpu.VMEM((rb, d), F32),
            pltpu.VMEM((rb, MIX_WIDTH), BF16),
            pltpu.VMEM((RET_HEADS, HEAD_DIM, HEAD_DIM), F32),
            pltpu.VMEM((ATT_KV_HEADS, WINDOW, HEAD_DIM), BF16),
            pltpu.VMEM((ATT_KV_HEADS, WINDOW, HEAD_DIM), BF16),
        ],
        compiler_params=pltpu.CompilerParams(
            dimension_semantics=("arbitrary",),
            vmem_limit_bytes=VMEM_LIMIT_BYTES,
        ),
    )(sinks, tables["chunk_decay"], x, gain, w_in, tables["cos_r"], tables["sin_r"],
      tables["cos_a"], tables["sin_lo"], tables["sin_hi"], q_gain, k_gain, tables["decay"],
      tables["q_head"], tables["k_tail"], w_out)


def _rope_tables(seq_len, dim, theta):
    inv = theta ** (-jnp.arange(0, dim, 2, dtype=F32) / dim)
    ang = jnp.arange(seq_len, dtype=F32)[:, None] * inv[None, :]
    ang = jnp.concatenate([ang, ang], axis=-1)
    return jnp.cos(ang), jnp.sin(ang)


def _position_tables(seq_len):
    cos_r, sin_r = _rope_tables(seq_len, HEAD_DIM, RET_THETA)
    hd2 = HEAD_DIM // 2
    sin_r = jnp.concatenate([-sin_r[:, :hd2], sin_r[:, hd2:]], axis=-1)

    cos_a, sin_a = _rope_tables(seq_len, ROPE_DIM, ROPE_THETA)
    half = ROPE_DIM // 2
    pad = lambda t, lo, fill: jnp.concatenate(
        [jnp.zeros((seq_len, lo), F32), t,
         jnp.full((seq_len, HEAD_DIM - lo - t.shape[1]), fill, F32)], axis=-1)
    cos_full = pad(cos_a, 0, 1.0)
    sin_lo = pad(-sin_a[:, :half], 0, 0.0)
    sin_hi = pad(sin_a[:, half:], half, 0.0)

    log_g = jnp.log1p(-jnp.exp2(-5.0 - jnp.arange(RET_HEADS, dtype=F32)))
    pos = jnp.arange(CHUNK, dtype=F32)
    diff = pos[:, None] - pos[None, :]
    decay = jnp.where(diff[None] >= 0,
                      jnp.exp(jnp.maximum(diff, 0.0)[None] * log_g[:, None, None]), 0.0)
    k_tail = jnp.exp((CHUNK - 1.0 - pos)[None, :] * log_g[:, None])
    q_head = jnp.exp((pos + 1.0)[None, :] * log_g[:, None])
    bcast = lambda t: jnp.broadcast_to(t[:, :, None], (RET_HEADS, CHUNK, HEAD_DIM))
    return {
        "cos_r": cos_r, "sin_r": sin_r, "cos_a": cos_full, "sin_lo": sin_lo, "sin_hi": sin_hi,
        "decay": decay, "k_tail": bcast(k_tail), "q_head": bcast(q_head),
        "chunk_decay": jnp.exp(CHUNK * log_g),
    }


def kernel(x, ffn1_norm, ffn1_w_gate, ffn1_w_up, ffn1_w_down, mix_norm, w_in, q_norm, k_norm,
           attn_sinks, w_out, ffn2_norm, ffn2_w_gate, ffn2_w_up, ffn2_w_down):
    b, s, d = x.shape
    depth = ffn1_norm.shape[0]
    tables = _position_tables(s)
    outs = []
    for bi in range(b):
        xb = x[bi]
        for l in range(depth):
            xb = _ffn(xb, ffn1_norm[l][None], ffn1_w_gate[l], ffn1_w_up[l], ffn1_w_down[l])
            xb = _mix_layer(xb, mix_norm[l][None], w_in[l].astype(BF16), w_out[l].astype(BF16),
                            attn_sinks[l], tables, q_norm[l][None], k_norm[l][None])
            xb = _ffn(xb, ffn2_norm[l][None], ffn2_w_gate[l], ffn2_w_up[l], ffn2_w_down[l])
        outs.append(xb)
    return jnp.stack(outs, axis=0)
```

```python
import functools

import jax
import jax.numpy as jnp
from jax import lax
from jax.experimental import pallas as pl
from jax.experimental.pallas import tpu as pltpu

D_MODEL = 2048
HEAD_DIM = 128
RET_HEADS = 8
ATT_Q_HEADS = 8
ATT_KV_HEADS = 2
GQA_GROUP = ATT_Q_HEADS // ATT_KV_HEADS
RET_WIDTH = RET_HEADS * HEAD_DIM
ATT_WIDTH = ATT_Q_HEADS * HEAD_DIM
KV_WIDTH = ATT_KV_HEADS * HEAD_DIM
MIX_WIDTH = RET_WIDTH + ATT_WIDTH
IN_WIDTH = 4 * RET_WIDTH + ATT_WIDTH + 2 * KV_WIDTH
CHUNK = 128
WINDOW = 128
ROPE_THETA = 500000.0
ROPE_DIM = HEAD_DIM // 4
RET_THETA = 10000.0
EPS = 1e-6

OFF_QR = 0
OFF_KR = RET_WIDTH
OFF_VR = 2 * RET_WIDTH
OFF_GR = 3 * RET_WIDTH
OFF_QA = 4 * RET_WIDTH
OFF_KA = OFF_QA + ATT_WIDTH
OFF_VA = OFF_KA + KV_WIDTH

VMEM_LIMIT_BYTES = 60 * 1024 * 1024

F32 = jnp.float32
BF16 = jnp.bfloat16


def _rms_rows(x, gain):
    ms = jnp.mean(x * x, axis=-1, keepdims=True)
    return x * lax.rsqrt(ms + EPS) * gain


def _ffn_kernel(x_hbm, gain_ref, wg_ref, wu_ref, wd_ref, o_ref, h_ref, xbuf_ref, sem,
                *, row_chunk):
    i = pl.program_id(0)
    j = pl.program_id(1)
    n_tiles = pl.num_programs(0)
    tm = o_ref.shape[0]

    def x_copy(tile):
        return pltpu.make_async_copy(x_hbm.at[pl.ds(tile * tm, tm), :], xbuf_ref, sem)

    @pl.when((i == 0) & (j == 0))
    def _():
        x_copy(0).start()

    @pl.when(j == 0)
    def _():
        x_copy(i).wait()

        def body(r, carry):
            rows = pl.ds(pl.multiple_of(r * row_chunk, row_chunk), row_chunk)
            x = xbuf_ref[rows, :]
            h_ref[rows, :] = _rms_rows(x, gain_ref[...]).astype(BF16)
            o_ref[rows, :] = x
            return carry

        lax.fori_loop(0, tm // row_chunk, body, 0)

    @pl.when((j == 1) & (i + 1 < n_tiles))
    def _():
        x_copy(i + 1).start()

    h = h_ref[...]
    g = jnp.dot(h, wg_ref[...].astype(BF16), preferred_element_type=F32)
    u = jnp.dot(h, wu_ref[...].astype(BF16), preferred_element_type=F32)
    a = (0.5 * (g * jax.nn.sigmoid(g)) * u).astype(BF16)
    o_ref[...] += jnp.dot(a, wd_ref[...].astype(BF16), preferred_element_type=F32)


def _ffn(x, gain, wg, wu, wd, *, tm=1024, tf=512, row_chunk=128):
    s, d = x.shape
    d_ff = wg.shape[1]
    return pl.pallas_call(
        functools.partial(_ffn_kernel, row_chunk=row_chunk),
        name="ffn",
        grid=(s // tm, d_ff // tf),
        in_specs=[
            pl.BlockSpec(memory_space=pl.ANY),
            pl.BlockSpec((1, d), lambda i, j: (0, 0)),
            pl.BlockSpec((d, tf), lambda i, j: (0, j)),
            pl.BlockSpec((d, tf), lambda i, j: (0, j)),
            pl.BlockSpec((tf, d), lambda i, j: (j, 0)),
        ],
        out_specs=pl.BlockSpec((tm, d), lambda i, j: (i, 0)),
        out_shape=jax.ShapeDtypeStruct((s, d), F32),
        scratch_shapes=[pltpu.VMEM((tm, d), BF16), pltpu.VMEM((tm, d), F32),
                        pltpu.SemaphoreType.DMA(())],
        compiler_params=pltpu.CompilerParams(
            dimension_semantics=("arbitrary", "arbitrary"),
            vmem_limit_bytes=VMEM_LIMIT_BYTES,
        ),
    )(x, gain, wg, wu, wd)


def _dot_nt(a, b):
    return lax.dot_general(a, b, (((1,), (1,)), ((), ())), preferred_element_type=F32)


def _dot_tn(a, b):
    return lax.dot_general(a, b, (((0,), (0,)), ((), ())), preferred_element_type=F32)


def _mix_chunk(zc, rows, not_first, sink_ref, cdecay_ref, tabs, qn_ref, kn_ref, decay_ref,
               qhead_ref, ktail_ref, y_ref, state_ref, kprev_ref, vprev_ref):
    cos_r_ref, sin_r_ref, cos_a_ref, sin_lo_ref, sin_hi_ref = tabs

    cos_r = cos_r_ref[rows, :]
    sin_r = sin_r_ref[rows, :]

    def rot_r(t):
        return t * cos_r + pltpu.roll(t, HEAD_DIM // 2, 1) * sin_r

    for h in range(RET_HEADS):
        q = rot_r(zc(OFF_QR, h))
        k = rot_r(zc(OFF_KR, h)) * (HEAD_DIM ** -0.5)
        vb = zc(OFF_VR, h).astype(BF16)
        scores = _dot_nt(q.astype(BF16), k.astype(BF16)) * decay_ref[h]
        intra = jnp.dot(scores.astype(BF16), vb, preferred_element_type=F32)
        chunk_kv = _dot_tn((k * ktail_ref[h]).astype(BF16), vb)
        prev = state_ref[h]
        cross = jnp.dot((q * qhead_ref[h]).astype(BF16), prev.astype(BF16),
                        preferred_element_type=F32)
        state_ref[h] = prev * cdecay_ref[h] + chunk_kv
        y = intra + cross
        y = y * lax.rsqrt(jnp.mean(y * y, axis=-1, keepdims=True) + EPS)
        g = zc(OFF_GR, h)
        y_ref[rows, h * HEAD_DIM:(h + 1) * HEAD_DIM] = ((g * jax.nn.sigmoid(g)) * y).astype(BF16)

    cos_a = cos_a_ref[rows, :]
    sin_lo = sin_lo_ref[rows, :]
    sin_hi = sin_hi_ref[rows, :]
    half = ROPE_DIM // 2

    def rot_a(t):
        return (t * cos_a + pltpu.roll(t, HEAD_DIM - half, 1) * sin_lo
                + pltpu.roll(t, half, 1) * sin_hi)

    qi = lax.broadcasted_iota(jnp.int32, (WINDOW, 2 * WINDOW), 0)
    kj = lax.broadcasted_iota(jnp.int32, (WINDOW, 2 * WINDOW), 1)
    rel = WINDOW + qi - kj
    mask = (rel >= 0) & (rel < WINDOW) & ((kj >= WINDOW) | not_first)
    neg = jnp.finfo(F32).min

    for kh in range(ATT_KV_HEADS):
        kb = rot_a(_rms_rows(zc(OFF_KA, kh), kn_ref[...])).astype(BF16)
        vb = zc(OFF_VA, kh).astype(BF16)
        kk = jnp.concatenate([kprev_ref[kh], kb], axis=0)
        vv = jnp.concatenate([vprev_ref[kh], vb], axis=0)
        for gq in range(GQA_GROUP):
            qh = kh * GQA_GROUP + gq
            qb = rot_a(_rms_rows(zc(OFF_QA, qh), qn_ref[...])).astype(BF16)
            s = _dot_nt(qb, kk) * (HEAD_DIM ** -0.5)
            s = jnp.where(mask, s, neg)
            sink = sink_ref[qh]
            m = jnp.maximum(jnp.max(s, axis=-1, keepdims=True), sink)
            p = jnp.exp(s - m)
            denom = jnp.sum(p, axis=-1, keepdims=True) + jnp.exp(sink - m)
            p = p / denom
            o = jnp.dot(p.astype(BF16), vv, preferred_element_type=F32)
            col = RET_WIDTH + qh * HEAD_DIM
            y_ref[rows, col:col + HEAD_DIM] = o.astype(BF16)
        kprev_ref[kh] = kb
        vprev_ref[kh] = vb


def _mix_layer_kernel(sink_ref, cdecay_ref, xn_ref, gain_ref, win_ref, cos_r_ref,
                      sin_r_ref, cos_a_ref, sin_lo_ref, sin_hi_ref, qn_ref, kn_ref, decay_ref,
                      qhead_ref, ktail_ref, wout_ref, o_ref,
                      z_ref, xprev_ref, y_ref, state_ref, kprev_ref, vprev_ref):
    s = pl.program_id(0)
    rb = xn_ref.shape[0]

    @pl.when(s == 0)
    def _():
        z_ref[1] = jnp.zeros(z_ref.shape[1:], F32)
        xprev_ref[...] = jnp.zeros_like(xprev_ref)

    @pl.when(s <= 1)
    def _():
        state_ref[...] = jnp.zeros_like(state_ref)
        kprev_ref[...] = jnp.zeros_like(kprev_ref)
        vprev_ref[...] = jnp.zeros_like(vprev_ref)

    w_slot = s % 2
    r_slot = 1 - w_slot

    h = _rms_rows(xn_ref[...], gain_ref[...]).astype(BF16)
    z_ref[w_slot] = jnp.dot(h, win_ref[...], preferred_element_type=F32)

    tabs = (cos_r_ref, sin_r_ref, cos_a_ref, sin_lo_ref, sin_hi_ref)
    for cc in range(rb // CHUNK):
        rows = slice(cc * CHUNK, (cc + 1) * CHUNK)

        def zc(off, hd, rows=rows):
            return z_ref[r_slot, rows, off + hd * HEAD_DIM: off + (hd + 1) * HEAD_DIM]

        not_first = (s > 1) if cc == 0 else True
        _mix_chunk(zc, rows, not_first, sink_ref, cdecay_ref, tabs, qn_ref, kn_ref, decay_ref,
                   qhead_ref, ktail_ref, y_ref, state_ref, kprev_ref, vprev_ref)

    o_ref[...] = xprev_ref[...] + jnp.dot(y_ref[...], wout_ref[...],
                                          preferred_element_type=F32)
    xprev_ref[...] = xn_ref[...]


def _mix_layer(x, gain, w_in, w_out, sinks, tables, q_gain, k_gain, *, rb=256):
    s, d = x.shape
    nb = s // rb
    cur = lambda t: (jnp.minimum(t, nb - 1), 0)
    prv = lambda t: (jnp.maximum(t - 1, 0), 0)
    const2 = lambda t: (0, 0)
    const3 = lambda t: (0, 0, 0)
    resident = dict(pipeline_mode=pl.Buffered(1))
    row_tab = pl.BlockSpec((rb, HEAD_DIM), prv)
    smem = pl.BlockSpec(memory_space=pltpu.SMEM)
    head_tab = pl.BlockSpec((RET_HEADS, CHUNK, HEAD_DIM), const3, **resident)
    return pl.pallas_call(
        _mix_layer_kernel,
        name="mix_layer",
        grid=(nb + 1,),
        in_specs=[
            smem, smem,
            pl.BlockSpec((rb, d), cur),
            pl.BlockSpec((1, d), const2),
            pl.BlockSpec((d, IN_WIDTH), const2, **resident),
            row_tab, row_tab, row_tab, row_tab, row_tab,
            pl.BlockSpec((1, HEAD_DIM), const2), pl.BlockSpec((1, HEAD_DIM), const2),
            head_tab, head_tab, head_tab,
            pl.BlockSpec((MIX_WIDTH, d), const2, **resident),
        ],
        out_specs=pl.BlockSpec((rb, d), prv),
        out_shape=jax.ShapeDtypeStruct((s, d), F32),
        scratch_shapes=[
            pltpu.VMEM((2, rb, IN_WIDTH), F32),
            pltpu.VMEM((rb, d), F32),
            pltpu.VMEM((rb, MIX_WIDTH), BF16),
            pltpu.VMEM((RET_HEADS, HEAD_DIM, HEAD_DIM), F32),
            pltpu.VMEM((ATT_KV_HEADS, WINDOW, HEAD_DIM), BF16),
            pltpu.VMEM((ATT_KV_HEADS, WINDOW, HEAD_DIM), BF16),
        ],
        compiler_params=pltpu.CompilerParams(
            dimension_semantics=("arbitrary",),
            vmem_limit_bytes=VMEM_LIMIT_BYTES,
        ),
    )(sinks, tables["chunk_decay"], x, gain, w_in, tables["cos_r"], tables["sin_r"],
      tables["cos_a"], tables["sin_lo"], tables["sin_hi"], q_gain, k_gain, tables["decay"],
      tables["q_head"], tables["k_tail"], w_out)


def _rope_tables(seq_len, dim, theta):
    inv = theta ** (-jnp.arange(0, dim, 2, dtype=F32) / dim)
    ang = jnp.arange(seq_len, dtype=F32)[:, None] * inv[None, :]
    ang = jnp.concatenate([ang, ang], axis=-1)
    return jnp.cos(ang), jnp.sin(ang)


def _position_tables(seq_len):
    cos_r, sin_r = _rope_tables(seq_len, HEAD_DIM, RET_THETA)
    hd2 = HEAD_DIM // 2
    sin_r = jnp.concatenate([-sin_r[:, :hd2], sin_r[:, hd2:]], axis=-1)

    cos_a, sin_a = _rope_tables(seq_len, ROPE_DIM, ROPE_THETA)
    half = ROPE_DIM // 2
    pad = lambda t, lo, fill: jnp.concatenate(
        [jnp.zeros((seq_len, lo), F32), t,
         jnp.full((seq_len, HEAD_DIM - lo - t.shape[1]), fill, F32)], axis=-1)
    cos_full = pad(cos_a, 0, 1.0)
    sin_lo = pad(-sin_a[:, :half], 0, 0.0)
    sin_hi = pad(sin_a[:, half:], half, 0.0)

    log_g = jnp.log1p(-jnp.exp2(-5.0 - jnp.arange(RET_HEADS, dtype=F32)))
    pos = jnp.arange(CHUNK, dtype=F32)
    diff = pos[:, None] - pos[None, :]
    decay = jnp.where(diff[None] >= 0,
                      jnp.exp(jnp.maximum(diff, 0.0)[None] * log_g[:, None, None]), 0.0)
    k_tail = jnp.exp((CHUNK - 1.0 - pos)[None, :] * log_g[:, None])
    q_head = jnp.exp((pos + 1.0)[None, :] * log_g[:, None])
    bcast = lambda t: jnp.broadcast_to(t[:, :, None], (RET_HEADS, CHUNK, HEAD_DIM))
    return {
        "cos_r": cos_r, "sin_r": sin_r, "cos_a": cos_full, "sin_lo": sin_lo, "sin_hi": sin_hi,
        "decay": decay, "k_tail": bcast(k_tail), "q_head": bcast(q_head),
        "chunk_decay": jnp.exp(CHUNK * log_g),
    }


def kernel(x, ffn1_norm, ffn1_w_gate, ffn1_w_up, ffn1_w_down, mix_norm, w_in, q_norm, k_norm,
           attn_sinks, w_out, ffn2_norm, ffn2_w_gate, ffn2_w_up, ffn2_w_down):
    b, s, d = x.shape
    depth = ffn1_norm.shape[0]
    tables = _position_tables(s)
    outs = []
    for bi in range(b):
        xb = x[bi]
        for l in range(depth):
            xb = _ffn(xb, ffn1_norm[l][None], ffn1_w_gate[l], ffn1_w_up[l], ffn1_w_down[l])
            xb = _mix_layer(xb, mix_norm[l][None], w_in[l].astype(BF16), w_out[l].astype(BF16),
                            attn_sinks[l], tables, q_norm[l][None], k_norm[l][None])
            xb = _ffn(xb, ffn2_norm[l][None], ffn2_w_gate[l], ffn2_w_up[l], ffn2_w_down[l])
        outs.append(xb)
    return jnp.stack(outs, axis=0)
```

```python
import functools
import itertools

import jax
import jax.numpy as jnp
import numpy as np
from jax import lax
from jax.experimental import pallas as pl
from jax.experimental.pallas import tpu as pltpu

D_MODEL = 2048
HEAD_DIM = 128
RET_HEADS = 8
ATT_Q_HEADS = 8
ATT_KV_HEADS = 2
GQA_GROUP = ATT_Q_HEADS // ATT_KV_HEADS
RET_WIDTH = RET_HEADS * HEAD_DIM
ATT_WIDTH = ATT_Q_HEADS * HEAD_DIM
KV_WIDTH = ATT_KV_HEADS * HEAD_DIM
MIX_WIDTH = RET_WIDTH + ATT_WIDTH
IN_WIDTH = 4 * RET_WIDTH + ATT_WIDTH + 2 * KV_WIDTH
CHUNK = 128
WINDOW = 128
ROPE_THETA = 500000.0
ROPE_DIM = HEAD_DIM // 4
RET_THETA = 10000.0
EPS = 1e-6

VMEM_LIMIT_BYTES = 62 * 1024 * 1024

F32 = jnp.float32
BF16 = jnp.bfloat16


def _rms_rows(x, gain):
    ms = jnp.mean(x * x, axis=-1, keepdims=True)
    return x * lax.rsqrt(ms + EPS) * gain


def _ffn_kernel(x_hbm, gain_ref, wg_ref, wu_ref, wd_ref, o_ref, h_ref, xbuf_ref, sem,
                *, row_chunk):
    i = pl.program_id(0)
    j = pl.program_id(1)
    n_tiles = pl.num_programs(0)
    tm = o_ref.shape[0]

    def x_copy(tile):
        return pltpu.make_async_copy(x_hbm.at[pl.ds(tile * tm, tm), :], xbuf_ref, sem)

    @pl.when((i == 0) & (j == 0))
    def _():
        x_copy(0).start()

    @pl.when(j == 0)
    def _():
        x_copy(i).wait()

        def body(r, carry):
            rows = pl.ds(pl.multiple_of(r * row_chunk, row_chunk), row_chunk)
            x = xbuf_ref[rows, :]
            h_ref[rows, :] = _rms_rows(x, gain_ref[...]).astype(BF16)
            o_ref[rows, :] = x
            return carry

        lax.fori_loop(0, tm // row_chunk, body, 0)

    @pl.when((j == 1) & (i + 1 < n_tiles))
    def _():
        x_copy(i + 1).start()

    h = h_ref[...]
    g = jnp.dot(h, wg_ref[...].astype(BF16), preferred_element_type=F32)
    u = jnp.dot(h, wu_ref[...].astype(BF16), preferred_element_type=F32)
    a = (0.5 * (g * jax.nn.sigmoid(g)) * u).astype(BF16)
    o_ref[...] += jnp.dot(a, wd_ref[...].astype(BF16), preferred_element_type=F32)


def _ffn(x, gain, wg, wu, wd, *, tm=1024, tf=512, row_chunk=128):
    s, d = x.shape
    d_ff = wg.shape[1]
    return pl.pallas_call(
        functools.partial(_ffn_kernel, row_chunk=row_chunk),
        name="ffn",
        grid=(s // tm, d_ff // tf),
        in_specs=[
            pl.BlockSpec(memory_space=pl.ANY),
            pl.BlockSpec((1, d), lambda i, j: (0, 0)),
            pl.BlockSpec((d, tf), lambda i, j: (0, j)),
            pl.BlockSpec((d, tf), lambda i, j: (0, j)),
            pl.BlockSpec((tf, d), lambda i, j: (j, 0)),
        ],
        out_specs=pl.BlockSpec((tm, d), lambda i, j: (i, 0)),
        out_shape=jax.ShapeDtypeStruct((s, d), F32),
        scratch_shapes=[pltpu.VMEM((tm, d), BF16), pltpu.VMEM((tm, d), F32),
                        pltpu.SemaphoreType.DMA(())],
        compiler_params=pltpu.CompilerParams(
            dimension_semantics=("arbitrary", "arbitrary"),
            vmem_limit_bytes=VMEM_LIMIT_BYTES,
        ),
    )(x, gain, wg, wu, wd)


_SEGMENTS = (("q_r", RET_WIDTH, False), ("k_r", RET_WIDTH, False), ("v_r", RET_WIDTH, True),
             ("g_r", RET_WIDTH, False), ("q_a", ATT_WIDTH, False), ("k_a", KV_WIDTH, False),
             ("v_a", KV_WIDTH, True))


def _z_layout():
    layout, w_off, widths = {}, 0, {False: 0, True: 0}
    for name, width, is_value in _SEGMENTS:
        layout[name] = (is_value, widths[is_value], w_off)
        widths[is_value] += width
        w_off += width
    return layout, widths[False], widths[True]


_Z_LAYOUT, ZF_WIDTH, ZV_WIDTH = _z_layout()


def _z_dest(w_col):
    for name, width, _ in _SEGMENTS:
        is_value, z_off, w_off = _Z_LAYOUT[name]
        if w_off <= w_col < w_off + width:
            return is_value, z_off + (w_col - w_off)
    raise ValueError(w_col)


def _dot_nt(a, b):
    return lax.dot_general(a, b, (((1,), (1,)), ((), ())), preferred_element_type=F32)


def _dot_tn(a, b):
    return lax.dot_general(a, b, (((0,), (0,)), ((), ())), preferred_element_type=F32)


LANE_SIGN_R, LANE_MASK_LO, LANE_MASK_HI = 0, 1, 2
OFF_COS_R, OFF_SIN_R, OFF_COS_A, OFF_SIN_A = 0, 1, 2, 3
GAM_LOG, GAM_CHUNK = 0, 1


def _mix_chunk(zhead, rows, not_first, sink_ref, gam_ref, base_refs, off, lane, qn_ref, kn_ref,
               y_ref, state_ref, kprev_ref, vprev_ref):
    cb_r_ref, sb_r_ref, cb_a_ref, sb_a_ref = base_refs
    row = lambda t, k: t[k:k + 1, :]

    cb, sb = cb_r_ref[rows, :], sb_r_ref[rows, :]
    co, so = row(off, OFF_COS_R), row(off, OFF_SIN_R)
    cos_r = cb * co - sb * so
    sin_r = (sb * co + cb * so) * row(lane, LANE_SIGN_R)

    def rot_r(t):
        return t * cos_r + pltpu.roll(t, HEAD_DIM // 2, 1) * sin_r

    n = lax.broadcasted_iota(jnp.int32, (CHUNK, CHUNK), 0)
    m = lax.broadcasted_iota(jnp.int32, (CHUNK, CHUNK), 1)
    causal = n >= m
    lag = jnp.maximum(n - m, 0).astype(F32)
    n_plus_1 = (n + 1).astype(F32)
    to_end = (CHUNK - 1 - n).astype(F32)

    for h in range(RET_HEADS):
        log_g = gam_ref[GAM_LOG, h]
        q = rot_r(zhead("q_r", h))
        k = rot_r(zhead("k_r", h)) * (HEAD_DIM ** -0.5)
        vb = zhead("v_r", h)
        decay = jnp.where(causal, jnp.exp(lag * log_g), 0.0)
        scores = _dot_nt(q.astype(BF16), k.astype(BF16)) * decay
        prev = state_ref[h]
        q_cross = (q * jnp.exp(n_plus_1 * log_g)).astype(BF16)
        lhs = jnp.concatenate([scores.astype(BF16), q_cross], axis=1)
        rhs = jnp.concatenate([vb, prev.astype(BF16)], axis=0)
        y = jnp.dot(lhs, rhs, preferred_element_type=F32)
        chunk_kv = _dot_tn((k * jnp.exp(to_end * log_g)).astype(BF16), vb)
        state_ref[h] = prev * gam_ref[GAM_CHUNK, h] + chunk_kv
        y = y * lax.rsqrt(jnp.mean(y * y, axis=-1, keepdims=True) + EPS)
        g = zhead("g_r", h)
        y_ref[rows, h * HEAD_DIM:(h + 1) * HEAD_DIM] = ((g * jax.nn.sigmoid(g)) * y).astype(BF16)
        yield

    cb, sb = cb_a_ref[rows, :], sb_a_ref[rows, :]
    co, so = row(off, OFF_COS_A), row(off, OFF_SIN_A)
    cos_a = cb * co - sb * so
    sin_a = sb * co + cb * so
    sin_lo = sin_a * row(lane, LANE_MASK_LO)
    sin_hi = sin_a * row(lane, LANE_MASK_HI)
    half = ROPE_DIM // 2

    def rot_a(t):
        return (t * cos_a + pltpu.roll(t, HEAD_DIM - half, 1) * sin_lo
                + pltpu.roll(t, half, 1) * sin_hi)

    qi = lax.broadcasted_iota(jnp.int32, (WINDOW, 2 * WINDOW), 0)
    kj = lax.broadcasted_iota(jnp.int32, (WINDOW, 2 * WINDOW), 1)
    rel = WINDOW + qi - kj
    mask = (rel >= 0) & (rel < WINDOW) & ((kj >= WINDOW) | not_first)
    neg = jnp.finfo(F32).min

    for kh in range(ATT_KV_HEADS):
        kb = rot_a(_rms_rows(zhead("k_a", kh), kn_ref[...])).astype(BF16)
        vb = zhead("v_a", kh)
        kk = jnp.concatenate([kprev_ref[kh], kb], axis=0)
        vv = jnp.concatenate([vprev_ref[kh], vb], axis=0)
        heads = range(kh * GQA_GROUP, (kh + 1) * GQA_GROUP)
        qs = jnp.concatenate(
            [rot_a(_rms_rows(zhead("q_a", qh), qn_ref[...])).astype(BF16) for qh in heads], axis=0)
        s_all = _dot_nt(qs, kk) * (HEAD_DIM ** -0.5)
        probs, inv_denoms = [], []
        for gq, qh in enumerate(heads):
            s = jnp.where(mask, s_all[gq * WINDOW:(gq + 1) * WINDOW, :], neg)
            sink = sink_ref[qh]
            mx = jnp.maximum(jnp.max(s, axis=-1, keepdims=True), sink)
            p = jnp.exp(s - mx)
            denom = jnp.sum(p, axis=-1, keepdims=True) + jnp.exp(sink - mx)
            probs.append(p.astype(BF16))
            inv_denoms.append(1.0 / denom)
        o_all = jnp.dot(jnp.concatenate(probs, axis=0), vv, preferred_element_type=F32)
        for gq, qh in enumerate(heads):
            o = o_all[gq * WINDOW:(gq + 1) * WINDOW, :] * inv_denoms[gq]
            col = RET_WIDTH + qh * HEAD_DIM
            y_ref[rows, col:col + HEAD_DIM] = o.astype(BF16)
        kprev_ref[kh] = kb
        vprev_ref[kh] = vb
        yield


def _mix_layer_kernel(sink_ref, gam_ref, xn_ref, gain_ref, win_ref, cb_r_ref, sb_r_ref,
                      cb_a_ref, sb_a_ref, off_ref, lane_ref, qn_ref, kn_ref, wout_ref, o_ref,
                      zfa_ref, zva_ref, zfb_ref, zvb_ref, h_ref, xprev_ref, y_ref, state_ref,
                      kprev_ref, vprev_ref, *, proj_tile):
    s = pl.program_id(0)
    rb = xn_ref.shape[0]

    @pl.when(s == 0)
    def _():
        zfb_ref[...] = jnp.zeros_like(zfb_ref)
        zvb_ref[...] = jnp.zeros_like(zvb_ref)
        xprev_ref[...] = jnp.zeros_like(xprev_ref)

    @pl.when(s <= 1)
    def _():
        state_ref[...] = jnp.zeros_like(state_ref)
        kprev_ref[...] = jnp.zeros_like(kprev_ref)
        vprev_ref[...] = jnp.zeros_like(vprev_ref)

    base_refs = (cb_r_ref, sb_r_ref, cb_a_ref, sb_a_ref)

    def step(zw, zr):
        off = off_ref[0]
        lane = lane_ref[...]

        def chunk_pieces(cc):
            rows = slice(cc * CHUNK, (cc + 1) * CHUNK)

            def zhead(name, hd):
                is_value, z_off, _ = _Z_LAYOUT[name]
                return zr[is_value][rows, z_off + hd * HEAD_DIM: z_off + (hd + 1) * HEAD_DIM]

            not_first = (s > 1) if cc == 0 else True
            return _mix_chunk(zhead, rows, not_first, sink_ref, gam_ref, base_refs, off, lane,
                              qn_ref, kn_ref, y_ref, state_ref, kprev_ref, vprev_ref)

        pieces = itertools.chain(*[chunk_pieces(cc) for cc in range(rb // CHUNK)])

        h_ref[...] = _rms_rows(xn_ref[...], gain_ref[...]).astype(BF16)
        for t in range(IN_WIDTH // proj_tile):
            w_col = t * proj_tile
            is_value, z_col = _z_dest(w_col)
            dst = zw[is_value]
            tile = jnp.dot(h_ref[...], win_ref[:, w_col:w_col + proj_tile],
                           preferred_element_type=F32)
            dst[:, z_col:z_col + proj_tile] = tile.astype(dst.dtype)
            next(pieces, None)
        for _ in pieces:
            pass

        o_ref[...] = xprev_ref[...] + jnp.dot(y_ref[...], wout_ref[...],
                                              preferred_element_type=F32)
        xprev_ref[...] = xn_ref[...]

    buf_a = {False: zfa_ref, True: zva_ref}
    buf_b = {False: zfb_ref, True: zvb_ref}

    @pl.when(s % 2 == 0)
    def _():
        step(buf_a, buf_b)

    @pl.when(s % 2 == 1)
    def _():
        step(buf_b, buf_a)


def _mix_layer(x, gain, w_in, w_out, sinks, q_gain, k_gain, *, rb=256, proj_tile=256):
    s, d = x.shape
    nb = s // rb
    t = _position_tables(rb, nb)
    cur = lambda i: (jnp.minimum(i, nb - 1), 0)
    prv = lambda i: (jnp.maximum(i - 1, 0), 0)
    const2 = lambda i: (0, 0)
    resident = dict(pipeline_mode=pl.Buffered(1))
    base_tab = pl.BlockSpec((rb, HEAD_DIM), const2, **resident)
    smem = pl.BlockSpec(memory_space=pltpu.SMEM)
    return pl.pallas_call(
        functools.partial(_mix_layer_kernel, proj_tile=proj_tile),
        name="mix_layer",
        grid=(nb + 1,),
        in_specs=[
            smem, smem,
            pl.BlockSpec((rb, d), cur),
            pl.BlockSpec((1, d), const2),
            pl.BlockSpec((d, IN_WIDTH), const2, **resident),
            base_tab, base_tab, base_tab, base_tab,
            pl.BlockSpec((1,) + t["block_off"].shape[1:], lambda i: (jnp.maximum(i - 1, 0), 0, 0)),
            pl.BlockSpec(t["lane"].shape, const2),
            pl.BlockSpec((1, HEAD_DIM), const2), pl.BlockSpec((1, HEAD_DIM), const2),
            pl.BlockSpec((MIX_WIDTH, d), const2, **resident),
        ],
        out_specs=pl.BlockSpec((rb, d), prv),
        out_shape=jax.ShapeDtypeStruct((s, d), F32),
        scratch_shapes=[
            pltpu.VMEM((rb, ZF_WIDTH), F32), pltpu.VMEM((rb, ZV_WIDTH), BF16),
            pltpu.VMEM((rb, ZF_WIDTH), F32), pltpu.VMEM((rb, ZV_WIDTH), BF16),
            pltpu.VMEM((rb, d), BF16),
            pltpu.VMEM((rb, d), F32),
            pltpu.VMEM((rb, MIX_WIDTH), BF16),
            pltpu.VMEM((RET_HEADS, HEAD_DIM, HEAD_DIM), F32),
            pltpu.VMEM((ATT_KV_HEADS, WINDOW, HEAD_DIM), BF16),
            pltpu.VMEM((ATT_KV_HEADS, WINDOW, HEAD_DIM), BF16),
        ],
        compiler_params=pltpu.CompilerParams(
            dimension_semantics=("arbitrary",),
            vmem_limit_bytes=VMEM_LIMIT_BYTES,
        ),
    )(sinks, t["gamma"], x, gain, w_in, t["cos_base_r"], t["sin_base_r"], t["cos_base_a"],
      t["sin_base_a"], t["block_off"], t["lane"], q_gain, k_gain, w_out)


def _position_tables(rb, nb):
    f32 = lambda a: jnp.asarray(np.asarray(a, np.float32))
    r = np.arange(rb, dtype=np.float64)[:, None]
    start = (np.arange(nb, dtype=np.float64) * rb)[:, None]

    inv_r = RET_THETA ** (-np.arange(0, HEAD_DIM, 2, dtype=np.float64) / HEAD_DIM)
    inv_r = np.concatenate([inv_r, inv_r])
    inv_a = ROPE_THETA ** (-np.arange(0, ROPE_DIM, 2, dtype=np.float64) / ROPE_DIM)
    inv_a = np.concatenate([inv_a, inv_a, np.zeros(HEAD_DIM - ROPE_DIM)])

    block_off = np.stack([np.cos(start * inv_r), np.sin(start * inv_r),
                          np.cos(start * inv_a), np.sin(start * inv_a)], axis=1)

    half = ROPE_DIM // 2
    lane = np.zeros((8, HEAD_DIM))
    lane[LANE_SIGN_R] = np.where(np.arange(HEAD_DIM) < HEAD_DIM // 2, -1.0, 1.0)
    lane[LANE_MASK_LO, :half] = -1.0
    lane[LANE_MASK_HI, half:ROPE_DIM] = 1.0

    log_g = np.log1p(-np.exp2(-5.0 - np.arange(RET_HEADS, dtype=np.float64)))
    gamma = np.stack([log_g, np.exp(CHUNK * log_g)])
    return {
        "cos_base_r": f32(np.cos(r * inv_r)), "sin_base_r": f32(np.sin(r * inv_r)),
        "cos_base_a": f32(np.cos(r * inv_a)), "sin_base_a": f32(np.sin(r * inv_a)),
        "block_off": f32(block_off), "lane": f32(lane), "gamma": f32(gamma),
    }


def kernel(x, ffn1_norm, ffn1_w_gate, ffn1_w_up, ffn1_w_down, mix_norm, w_in, q_norm, k_norm,
           attn_sinks, w_out, ffn2_norm, ffn2_w_gate, ffn2_w_up, ffn2_w_down):
    b, s, d = x.shape
    depth = ffn1_norm.shape[0]
    outs = []
    for bi in range(b):
        xb = x[bi]
        for l in range(depth):
            xb = _ffn(xb, ffn1_norm[l][None], ffn1_w_gate[l], ffn1_w_up[l], ffn1_w_down[l])
            xb = _mix_layer(xb, mix_norm[l][None], w_in[l].astype(BF16), w_out[l].astype(BF16),
                            attn_sinks[l], q_norm[l][None], k_norm[l][None])
            xb = _ffn(xb, ffn2_norm[l][None], ffn2_w_gate[l], ffn2_w_up[l], ffn2_w_down[l])
        outs.append(xb)
    return jnp.stack(outs, axis=0)
```

```python
import functools
import itertools

import jax
import jax.numpy as jnp
import numpy as np
from jax import lax
from jax.experimental import pallas as pl
from jax.experimental.pallas import tpu as pltpu

D_MODEL = 2048
HEAD_DIM = 128
RET_HEADS = 8
ATT_Q_HEADS = 8
ATT_KV_HEADS = 2
GQA_GROUP = ATT_Q_HEADS // ATT_KV_HEADS
RET_WIDTH = RET_HEADS * HEAD_DIM
ATT_WIDTH = ATT_Q_HEADS * HEAD_DIM
KV_WIDTH = ATT_KV_HEADS * HEAD_DIM
MIX_WIDTH = RET_WIDTH + ATT_WIDTH
IN_WIDTH = 4 * RET_WIDTH + ATT_WIDTH + 2 * KV_WIDTH
CHUNK = 128
WINDOW = 128
ROPE_THETA = 500000.0
ROPE_DIM = HEAD_DIM // 4
RET_THETA = 10000.0
EPS = 1e-6

VMEM_LIMIT_BYTES = 62 * 1024 * 1024

F32 = jnp.float32
BF16 = jnp.bfloat16


def _rms_rows(x, gain):
    ms = jnp.mean(x * x, axis=-1, keepdims=True)
    return x * lax.rsqrt(ms + EPS) * gain


def _ffn_kernel(x_hbm, gain_ref, wg_ref, wu_ref, wd_ref, o_ref, h_ref, xbuf_ref, sem,
                *, row_chunk):
    i = pl.program_id(0)
    j = pl.program_id(1)
    n_tiles = pl.num_programs(0)
    tm = o_ref.shape[0]

    def x_copy(tile):
        return pltpu.make_async_copy(x_hbm.at[pl.ds(tile * tm, tm), :], xbuf_ref, sem)

    @pl.when((i == 0) & (j == 0))
    def _():
        x_copy(0).start()

    @pl.when(j == 0)
    def _():
        x_copy(i).wait()

        def body(r, carry):
            rows = pl.ds(pl.multiple_of(r * row_chunk, row_chunk), row_chunk)
            x = xbuf_ref[rows, :]
            h_ref[rows, :] = _rms_rows(x, gain_ref[...]).astype(BF16)
            o_ref[rows, :] = x
            return carry

        lax.fori_loop(0, tm // row_chunk, body, 0)

    @pl.when((j == 1) & (i + 1 < n_tiles))
    def _():
        x_copy(i + 1).start()

    h = h_ref[...]
    g = jnp.dot(h, wg_ref[...].astype(BF16), preferred_element_type=F32)
    u = jnp.dot(h, wu_ref[...].astype(BF16), preferred_element_type=F32)
    a = (0.5 * (g * jax.nn.sigmoid(g)) * u).astype(BF16)
    o_ref[...] += jnp.dot(a, wd_ref[...].astype(BF16), preferred_element_type=F32)


def _ffn(x, gain, wg, wu, wd, *, tm=1024, tf=512, row_chunk=128):
    s, d = x.shape
    d_ff = wg.shape[1]
    return pl.pallas_call(
        functools.partial(_ffn_kernel, row_chunk=row_chunk),
        name="ffn",
        grid=(s // tm, d_ff // tf),
        in_specs=[
            pl.BlockSpec(memory_space=pl.ANY),
            pl.BlockSpec((1, d), lambda i, j: (0, 0)),
            pl.BlockSpec((d, tf), lambda i, j: (0, j)),
            pl.BlockSpec((d, tf), lambda i, j: (0, j)),
            pl.BlockSpec((tf, d), lambda i, j: (j, 0)),
        ],
        out_specs=pl.BlockSpec((tm, d), lambda i, j: (i, 0)),
        out_shape=jax.ShapeDtypeStruct((s, d), F32),
        scratch_shapes=[pltpu.VMEM((tm, d), BF16), pltpu.VMEM((tm, d), F32),
                        pltpu.SemaphoreType.DMA(())],
        compiler_params=pltpu.CompilerParams(
            dimension_semantics=("arbitrary", "arbitrary"),
            vmem_limit_bytes=VMEM_LIMIT_BYTES,
        ),
    )(x, gain, wg, wu, wd)


_SEGMENTS = (("q_r", RET_WIDTH, False), ("k_r", RET_WIDTH, False), ("v_r", RET_WIDTH, True),
             ("g_r", RET_WIDTH, False), ("q_a", ATT_WIDTH, False), ("k_a", KV_WIDTH, False),
             ("v_a", KV_WIDTH, True))


def _z_layout():
    layout, w_off, widths = {}, 0, {False: 0, True: 0}
    for name, width, is_value in _SEGMENTS:
        layout[name] = (is_value, widths[is_value], w_off)
        widths[is_value] += width
        w_off += width
    return layout, widths[False], widths[True]


_Z_LAYOUT, ZF_WIDTH, ZV_WIDTH = _z_layout()


def _z_dest(w_col):
    for name, width, _ in _SEGMENTS:
        is_value, z_off, w_off = _Z_LAYOUT[name]
        if w_off <= w_col < w_off + width:
            return is_value, z_off + (w_col - w_off)
    raise ValueError(w_col)


def _dot_nt(a, b):
    return lax.dot_general(a, b, (((1,), (1,)), ((), ())), preferred_element_type=F32)


def _dot_tn(a, b):
    return lax.dot_general(a, b, (((0,), (0,)), ((), ())), preferred_element_type=F32)


LANE_SIGN_R, LANE_MASK_LO, LANE_MASK_HI = 0, 1, 2
OFF_COS_R, OFF_SIN_R, OFF_COS_A, OFF_SIN_A = 0, 1, 2, 3
GAM_LOG, GAM_CHUNK = 0, 1


def _mix_chunk(zhead, rows, not_first, sink_ref, gam_ref, base_refs, off, lane, qn_ref, kn_ref,
               y_ref, state_ref, kprev_ref, vprev_ref):
    cb_r_ref, sb_r_ref, cb_a_ref, sb_a_ref = base_refs
    row = lambda t, k: t[k:k + 1, :]

    cb, sb = cb_r_ref[rows, :], sb_r_ref[rows, :]
    co, so = row(off, OFF_COS_R), row(off, OFF_SIN_R)
    cos_r = cb * co - sb * so
    sin_r = (sb * co + cb * so) * row(lane, LANE_SIGN_R)

    def rot_r(t):
        return t * cos_r + pltpu.roll(t, HEAD_DIM // 2, 1) * sin_r

    n = lax.broadcasted_iota(jnp.int32, (CHUNK, CHUNK), 0)
    m = lax.broadcasted_iota(jnp.int32, (CHUNK, CHUNK), 1)
    causal = n >= m
    lag = jnp.maximum(n - m, 0).astype(F32)
    n_plus_1 = (n + 1).astype(F32)
    to_end = (CHUNK - 1 - n).astype(F32)

    for h in range(RET_HEADS):
        log_g = gam_ref[GAM_LOG, h]
        q = rot_r(zhead("q_r", h))
        k = rot_r(zhead("k_r", h)) * (HEAD_DIM ** -0.5)
        vb = zhead("v_r", h)
        decay = jnp.where(causal, jnp.exp(lag * log_g), 0.0)
        scores = _dot_nt(q.astype(BF16), k.astype(BF16)) * decay
        prev = state_ref[h]
        q_cross = (q * jnp.exp(n_plus_1 * log_g)).astype(BF16)
        lhs = jnp.concatenate([scores.astype(BF16), q_cross], axis=1)
        rhs = jnp.concatenate([vb, prev.astype(BF16)], axis=0)
        y = jnp.dot(lhs, rhs, preferred_element_type=F32)
        chunk_kv = _dot_tn((k * jnp.exp(to_end * log_g)).astype(BF16), vb)
        state_ref[h] = prev * gam_ref[GAM_CHUNK, h] + chunk_kv
        y = y * lax.rsqrt(jnp.mean(y * y, axis=-1, keepdims=True) + EPS)
        g = zhead("g_r", h)
        y_ref[rows, h * HEAD_DIM:(h + 1) * HEAD_DIM] = ((g * jax.nn.sigmoid(g)) * y).astype(BF16)
        yield

    cb, sb = cb_a_ref[rows, :], sb_a_ref[rows, :]
    co, so = row(off, OFF_COS_A), row(off, OFF_SIN_A)
    cos_a = cb * co - sb * so
    sin_a = sb * co + cb * so
    sin_lo = sin_a * row(lane, LANE_MASK_LO)
    sin_hi = sin_a * row(lane, LANE_MASK_HI)
    half = ROPE_DIM // 2

    def rot_a(t):
        return (t * cos_a + pltpu.roll(t, HEAD_DIM - half, 1) * sin_lo
                + pltpu.roll(t, half, 1) * sin_hi)

    qi = lax.broadcasted_iota(jnp.int32, (WINDOW, 2 * WINDOW), 0)
    kj = lax.broadcasted_iota(jnp.int32, (WINDOW, 2 * WINDOW), 1)
    rel = WINDOW + qi - kj
    mask = (rel >= 0) & (rel < WINDOW) & ((kj >= WINDOW) | not_first)
    neg = jnp.finfo(F32).min

    for kh in range(ATT_KV_HEADS):
        kb = rot_a(_rms_rows(zhead("k_a", kh), kn_ref[...])).astype(BF16)
        vb = zhead("v_a", kh)
        kk = jnp.concatenate([kprev_ref[kh], kb], axis=0)
        vv = jnp.concatenate([vprev_ref[kh], vb], axis=0)
        heads = range(kh * GQA_GROUP, (kh + 1) * GQA_GROUP)
        qs = jnp.concatenate(
            [rot_a(_rms_rows(zhead("q_a", qh), qn_ref[...])).astype(BF16) for qh in heads], axis=0)
        s_all = _dot_nt(qs, kk) * (HEAD_DIM ** -0.5)
        probs, inv_denoms = [], []
        for gq, qh in enumerate(heads):
            s = jnp.where(mask, s_all[gq * WINDOW:(gq + 1) * WINDOW, :], neg)
            sink = sink_ref[qh]
            mx = jnp.maximum(jnp.max(s, axis=-1, keepdims=True), sink)
            p = jnp.exp(s - mx)
            denom = jnp.sum(p, axis=-1, keepdims=True) + jnp.exp(sink - mx)
            probs.append(p.astype(BF16))
            inv_denoms.append(1.0 / denom)
        o_all = jnp.dot(jnp.concatenate(probs, axis=0), vv, preferred_element_type=F32)
        for gq, qh in enumerate(heads):
            o = o_all[gq * WINDOW:(gq + 1) * WINDOW, :] * inv_denoms[gq]
            col = RET_WIDTH + qh * HEAD_DIM
            y_ref[rows, col:col + HEAD_DIM] = o.astype(BF16)
        kprev_ref[kh] = kb
        vprev_ref[kh] = vb
        yield


def _load_weights_as_bf16(win_hbm, wout_hbm, win_ref, wout_ref, stage_ref, sem):
    rows = stage_ref.shape[0]
    slot_w = wout_hbm.shape[1]
    assert stage_ref.shape[1] >= 2 * slot_w and win_hbm.shape[0] == wout_hbm.shape[0]
    slabs = [(win_hbm, win_ref, c0, min(slot_w, win_hbm.shape[1] - c0))
             for c0 in range(0, win_hbm.shape[1], slot_w)]
    slabs.append((wout_hbm, wout_ref, 0, slot_w))
    assert len(slabs) % 2 == 0
    n_blocks = win_hbm.shape[0] // rows

    def copy(r, p):
        src, _, c0, width = slabs[p]
        r0 = pl.multiple_of(r * rows, rows)
        return pltpu.make_async_copy(
            src.at[pl.ds(r0, rows), pl.ds(c0, width)],
            stage_ref.at[:, pl.ds((p % 2) * slot_w, width)],
            sem.at[p % 2])

    copy(0, 0).start()

    def body(r, carry):
        for p, (_, dst, c0, width) in enumerate(slabs):
            if p + 1 < len(slabs):
                copy(r, p + 1).start()
            else:
                @pl.when(r + 1 < n_blocks)
                def _():
                    copy(r + 1, 0).start()
            copy(r, p).wait()
            r0 = pl.multiple_of(r * rows, rows)
            lo = (p % 2) * slot_w
            dst[pl.ds(r0, rows), c0:c0 + width] = stage_ref[:, lo:lo + width].astype(BF16)
        return carry

    lax.fori_loop(0, n_blocks, body, 0)


def _mix_layer_kernel(sink_ref, gam_ref, xn_ref, gain_ref, win_hbm, cb_r_ref, sb_r_ref,
                      cb_a_ref, sb_a_ref, off_ref, lane_ref, qn_ref, kn_ref, wout_hbm, o_ref,
                      zfa_ref, zva_ref, zfb_ref, zvb_ref, h_ref, xprev_ref, y_ref, state_ref,
                      kprev_ref, vprev_ref, win_ref, wout_ref, wsem, *, proj_tile):
    s = pl.program_id(0)
    rb = xn_ref.shape[0]

    @pl.when(s == 0)
    def _():
        _load_weights_as_bf16(win_hbm, wout_hbm, win_ref, wout_ref, zfb_ref, wsem)
        zfb_ref[...] = jnp.zeros_like(zfb_ref)
        zvb_ref[...] = jnp.zeros_like(zvb_ref)
        xprev_ref[...] = jnp.zeros_like(xprev_ref)

    @pl.when(s <= 1)
    def _():
        state_ref[...] = jnp.zeros_like(state_ref)
        kprev_ref[...] = jnp.zeros_like(kprev_ref)
        vprev_ref[...] = jnp.zeros_like(vprev_ref)

    base_refs = (cb_r_ref, sb_r_ref, cb_a_ref, sb_a_ref)

    def step(zw, zr):
        off = off_ref[0]
        lane = lane_ref[...]

        def chunk_pieces(cc):
            rows = slice(cc * CHUNK, (cc + 1) * CHUNK)

            def zhead(name, hd):
                is_value, z_off, _ = _Z_LAYOUT[name]
                return zr[is_value][rows, z_off + hd * HEAD_DIM: z_off + (hd + 1) * HEAD_DIM]

            not_first = (s > 1) if cc == 0 else True
            return _mix_chunk(zhead, rows, not_first, sink_ref, gam_ref, base_refs, off, lane,
                              qn_ref, kn_ref, y_ref, state_ref, kprev_ref, vprev_ref)

        pieces = itertools.chain(*[chunk_pieces(cc) for cc in range(rb // CHUNK)])

        h_ref[...] = _rms_rows(xn_ref[...], gain_ref[...]).astype(BF16)
        for t in range(IN_WIDTH // proj_tile):
            w_col = t * proj_tile
            is_value, z_col = _z_dest(w_col)
            dst = zw[is_value]
            tile = jnp.dot(h_ref[...], win_ref[:, w_col:w_col + proj_tile],
                           preferred_element_type=F32)
            dst[:, z_col:z_col + proj_tile] = tile.astype(dst.dtype)
            next(pieces, None)
        for _ in pieces:
            pass

        o_ref[...] = xprev_ref[...] + jnp.dot(y_ref[...], wout_ref[...],
                                              preferred_element_type=F32)
        xprev_ref[...] = xn_ref[...]

    buf_a = {False: zfa_ref, True: zva_ref}
    buf_b = {False: zfb_ref, True: zvb_ref}

    @pl.when(s % 2 == 0)
    def _():
        step(buf_a, buf_b)

    @pl.when(s % 2 == 1)
    def _():
        step(buf_b, buf_a)


def _mix_layer(x, gain, w_in, w_out, sinks, q_gain, k_gain, *, rb=256, proj_tile=256):
    s, d = x.shape
    nb = s // rb
    t = _position_tables(rb, nb)
    cur = lambda i: (jnp.minimum(i, nb - 1), 0)
    prv = lambda i: (jnp.maximum(i - 1, 0), 0)
    const2 = lambda i: (0, 0)
    resident = dict(pipeline_mode=pl.Buffered(1))
    base_tab = pl.BlockSpec((rb, HEAD_DIM), const2, **resident)
    smem = pl.BlockSpec(memory_space=pltpu.SMEM)
    return pl.pallas_call(
        functools.partial(_mix_layer_kernel, proj_tile=proj_tile),
        name="mix_layer",
        grid=(nb + 1,),
        in_specs=[
            smem, smem,
            pl.BlockSpec((rb, d), cur),
            pl.BlockSpec((1, d), const2),
            pl.BlockSpec(memory_space=pl.ANY),
            base_tab, base_tab, base_tab, base_tab,
            pl.BlockSpec((1,) + t["block_off"].shape[1:], lambda i: (jnp.maximum(i - 1, 0), 0, 0)),
            pl.BlockSpec(t["lane"].shape, const2),
            pl.BlockSpec((1, HEAD_DIM), const2), pl.BlockSpec((1, HEAD_DIM), const2),
            pl.BlockSpec(memory_space=pl.ANY),
        ],
        out_specs=pl.BlockSpec((rb, d), prv),
        out_shape=jax.ShapeDtypeStruct((s, d), F32),
        scratch_shapes=[
            pltpu.VMEM((rb, ZF_WIDTH), F32), pltpu.VMEM((rb, ZV_WIDTH), BF16),
            pltpu.VMEM((rb, ZF_WIDTH), F32), pltpu.VMEM((rb, ZV_WIDTH), BF16),
            pltpu.VMEM((rb, d), BF16),
            pltpu.VMEM((rb, d), F32),
            pltpu.VMEM((rb, MIX_WIDTH), BF16),
            pltpu.VMEM((RET_HEADS, HEAD_DIM, HEAD_DIM), F32),
            pltpu.VMEM((ATT_KV_HEADS, WINDOW, HEAD_DIM), BF16),
            pltpu.VMEM((ATT_KV_HEADS, WINDOW, HEAD_DIM), BF16),
            pltpu.VMEM((d, IN_WIDTH), BF16),
            pltpu.VMEM((MIX_WIDTH, d), BF16),
            pltpu.SemaphoreType.DMA((2,)),
        ],
        compiler_params=pltpu.CompilerParams(
            dimension_semantics=("arbitrary",),
            vmem_limit_bytes=VMEM_LIMIT_BYTES,
        ),
    )(sinks, t["gamma"], x, gain, w_in, t["cos_base_r"], t["sin_base_r"], t["cos_base_a"],
      t["sin_base_a"], t["block_off"], t["lane"], q_gain, k_gain, w_out)


def _position_tables(rb, nb):
    f32 = lambda a: jnp.asarray(np.asarray(a, np.float32))
    r = np.arange(rb, dtype=np.float64)[:, None]
    start = (np.arange(nb, dtype=np.float64) * rb)[:, None]

    inv_r = RET_THETA ** (-np.arange(0, HEAD_DIM, 2, dtype=np.float64) / HEAD_DIM)
    inv_r = np.concatenate([inv_r, inv_r])
    inv_a = ROPE_THETA ** (-np.arange(0, ROPE_DIM, 2, dtype=np.float64) / ROPE_DIM)
    inv_a = np.concatenate([inv_a, inv_a, np.zeros(HEAD_DIM - ROPE_DIM)])

    block_off = np.stack([np.cos(start * inv_r), np.sin(start * inv_r),
                          np.cos(start * inv_a), np.sin(start * inv_a)], axis=1)

    half = ROPE_DIM // 2
    lane = np.zeros((8, HEAD_DIM))
    lane[LANE_SIGN_R] = np.where(np.arange(HEAD_DIM) < HEAD_DIM // 2, -1.0, 1.0)
    lane[LANE_MASK_LO, :half] = -1.0
    lane[LANE_MASK_HI, half:ROPE_DIM] = 1.0

    log_g = np.log1p(-np.exp2(-5.0 - np.arange(RET_HEADS, dtype=np.float64)))
    gamma = np.stack([log_g, np.exp(CHUNK * log_g)])
    return {
        "cos_base_r": f32(np.cos(r * inv_r)), "sin_base_r": f32(np.sin(r * inv_r)),
        "cos_base_a": f32(np.cos(r * inv_a)), "sin_base_a": f32(np.sin(r * inv_a)),
        "block_off": f32(block_off), "lane": f32(lane), "gamma": f32(gamma),
    }


def kernel(x, ffn1_norm, ffn1_w_gate, ffn1_w_up, ffn1_w_down, mix_norm, w_in, q_norm, k_norm,
           attn_sinks, w_out, ffn2_norm, ffn2_w_gate, ffn2_w_up, ffn2_w_down):
    b, s, d = x.shape
    depth = ffn1_norm.shape[0]
    outs = []
    for bi in range(b):
        xb = x[bi]
        for l in range(depth):
            xb = _ffn(xb, ffn1_norm[l][None], ffn1_w_gate[l], ffn1_w_up[l], ffn1_w_down[l])
            xb = _mix_layer(xb, mix_norm[l][None], w_in[l], w_out[l], attn_sinks[l],
                            q_norm[l][None], k_norm[l][None])
            xb = _ffn(xb, ffn2_norm[l][None], ffn2_w_gate[l], ffn2_w_up[l], ffn2_w_down[l])
        outs.append(xb)
    return jnp.stack(outs, axis=0)
```

```python
import functools
import itertools

import jax
import jax.numpy as jnp
import numpy as np
from jax import lax
from jax.experimental import pallas as pl
from jax.experimental.pallas import tpu as pltpu

D_MODEL = 2048
HEAD_DIM = 128
RET_HEADS = 8
ATT_Q_HEADS = 8
ATT_KV_HEADS = 2
GQA_GROUP = ATT_Q_HEADS // ATT_KV_HEADS
RET_WIDTH = RET_HEADS * HEAD_DIM
ATT_WIDTH = ATT_Q_HEADS * HEAD_DIM
KV_WIDTH = ATT_KV_HEADS * HEAD_DIM
MIX_WIDTH = RET_WIDTH + ATT_WIDTH
IN_WIDTH = 4 * RET_WIDTH + ATT_WIDTH + 2 * KV_WIDTH
CHUNK = 128
WINDOW = 128
ROPE_THETA = 500000.0
ROPE_DIM = HEAD_DIM // 4
RET_THETA = 10000.0
EPS = 1e-6

VMEM_LIMIT_BYTES = 62 * 1024 * 1024

F32 = jnp.float32
BF16 = jnp.bfloat16


def _rms_rows(x, gain):
    ms = jnp.mean(x * x, axis=-1, keepdims=True)
    return x * lax.rsqrt(ms + EPS) * gain


def _ffn_kernel(x_hbm, gain_ref, wg_ref, wu_ref, wd_ref, o_ref, h_ref, xbuf_ref, sem,
                *, row_chunk):
    i = pl.program_id(0)
    j = pl.program_id(1)
    n_tiles = pl.num_programs(0)
    last_j = pl.num_programs(1) - 1
    tm = o_ref.shape[0]
    slot = i % 2

    def x_copy(tile):
        return pltpu.make_async_copy(x_hbm.at[pl.ds(tile * tm, tm), :], xbuf_ref, sem)

    def norm_rows(r, dst_slot):
        rows = pl.ds(r * row_chunk, row_chunk)
        h_ref[dst_slot, rows, :] = _rms_rows(xbuf_ref[rows, :], gain_ref[...]).astype(BF16)

    @pl.when((i == 0) & (j == 0))
    def _():
        x_copy(0).start()
        x_copy(0).wait()
        for r in range(tm // row_chunk):
            norm_rows(r, 0)

    @pl.when(j == 0)
    def _():
        o_ref[...] = xbuf_ref[...]

    @pl.when(j == 1)
    def _():
        x_copy(jnp.minimum(i + 1, n_tiles - 1)).start()

    def matmuls():
        h = h_ref[slot]
        g = jnp.dot(h, wg_ref[...].astype(BF16), preferred_element_type=F32)
        u = jnp.dot(h, wu_ref[...].astype(BF16), preferred_element_type=F32)
        a = (0.5 * (g * jax.nn.sigmoid(g)) * u).astype(BF16)
        o_ref[...] += jnp.dot(a, wd_ref[...].astype(BF16), preferred_element_type=F32)

    @pl.when(j < last_j)
    def _():
        matmuls()

    @pl.when(j == last_j)
    def _():
        x_copy(i).wait()
        matmuls()
        for r in range(tm // row_chunk):
            norm_rows(r, 1 - slot)


def _ffn(x, gain, wg, wu, wd, *, tm=1024, tf=512, row_chunk=128):
    s, d = x.shape
    d_ff = wg.shape[1]
    return pl.pallas_call(
        functools.partial(_ffn_kernel, row_chunk=row_chunk),
        name="ffn",
        grid=(s // tm, d_ff // tf),
        in_specs=[
            pl.BlockSpec(memory_space=pl.ANY),
            pl.BlockSpec((1, d), lambda i, j: (0, 0)),
            pl.BlockSpec((d, tf), lambda i, j: (0, j)),
            pl.BlockSpec((d, tf), lambda i, j: (0, j)),
            pl.BlockSpec((tf, d), lambda i, j: (j, 0)),
        ],
        out_specs=pl.BlockSpec((tm, d), lambda i, j: (i, 0)),
        out_shape=jax.ShapeDtypeStruct((s, d), F32),
        scratch_shapes=[pltpu.VMEM((2, tm, d), BF16), pltpu.VMEM((tm, d), F32),
                        pltpu.SemaphoreType.DMA(())],
        compiler_params=pltpu.CompilerParams(
            dimension_semantics=("arbitrary", "arbitrary"),
            vmem_limit_bytes=VMEM_LIMIT_BYTES,
        ),
    )(x, gain, wg, wu, wd)


_SEGMENTS = (("q_r", RET_WIDTH, False), ("k_r", RET_WIDTH, False), ("v_r", RET_WIDTH, True),
             ("g_r", RET_WIDTH, False), ("q_a", ATT_WIDTH, False), ("k_a", KV_WIDTH, False),
             ("v_a", KV_WIDTH, True))


def _z_layout():
    layout, w_off, widths = {}, 0, {False: 0, True: 0}
    for name, width, is_value in _SEGMENTS:
        layout[name] = (is_value, widths[is_value], w_off)
        widths[is_value] += width
        w_off += width
    return layout, widths[False], widths[True]


_Z_LAYOUT, ZF_WIDTH, ZV_WIDTH = _z_layout()


def _z_dest(w_col):
    for name, width, _ in _SEGMENTS:
        is_value, z_off, w_off = _Z_LAYOUT[name]
        if w_off <= w_col < w_off + width:
            return is_value, z_off + (w_col - w_off)
    raise ValueError(w_col)


def _dot_nt(a, b):
    return lax.dot_general(a, b, (((1,), (1,)), ((), ())), preferred_element_type=F32)


def _dot_tn(a, b):
    return lax.dot_general(a, b, (((0,), (0,)), ((), ())), preferred_element_type=F32)


LANE_SIGN_R, LANE_MASK_LO, LANE_MASK_HI = 0, 1, 2
OFF_COS_R, OFF_SIN_R, OFF_COS_A, OFF_SIN_A = 0, 1, 2, 3
GAM_LOG, GAM_CHUNK = 0, 1


def _mix_chunk(zhead, rows, not_first, sink_ref, gam_ref, base_refs, off, lane, qn_ref, kn_ref,
               y_ref, state_ref, kprev_ref, vprev_ref):
    cb_r_ref, sb_r_ref, cb_a_ref, sb_a_ref = base_refs
    row = lambda t, k: t[k:k + 1, :]

    cb, sb = cb_r_ref[rows, :], sb_r_ref[rows, :]
    co, so = row(off, OFF_COS_R), row(off, OFF_SIN_R)
    cos_r = cb * co - sb * so
    sin_r = (sb * co + cb * so) * row(lane, LANE_SIGN_R)

    def rot_r(t):
        return t * cos_r + pltpu.roll(t, HEAD_DIM // 2, 1) * sin_r

    n = lax.broadcasted_iota(jnp.int32, (CHUNK, CHUNK), 0)
    m = lax.broadcasted_iota(jnp.int32, (CHUNK, CHUNK), 1)
    causal = n >= m
    lag = jnp.maximum(n - m, 0).astype(F32)
    n_plus_1 = (n + 1).astype(F32)
    to_end = (CHUNK - 1 - n).astype(F32)

    for h in range(RET_HEADS):
        log_g = gam_ref[GAM_LOG, h]
        q = rot_r(zhead("q_r", h))
        k = rot_r(zhead("k_r", h)) * (HEAD_DIM ** -0.5)
        vb = zhead("v_r", h)
        decay = jnp.where(causal, jnp.exp(lag * log_g), 0.0)
        scores = _dot_nt(q.astype(BF16), k.astype(BF16)) * decay
        prev = state_ref[h]
        q_cross = (q * jnp.exp(n_plus_1 * log_g)).astype(BF16)
        lhs = jnp.concatenate([scores.astype(BF16), q_cross], axis=1)
        rhs = jnp.concatenate([vb, prev.astype(BF16)], axis=0)
        y = jnp.dot(lhs, rhs, preferred_element_type=F32)
        chunk_kv = _dot_tn((k * jnp.exp(to_end * log_g)).astype(BF16), vb)
        state_ref[h] = prev * gam_ref[GAM_CHUNK, h] + chunk_kv
        y = y * lax.rsqrt(jnp.mean(y * y, axis=-1, keepdims=True) + EPS)
        g = zhead("g_r", h)
        y_ref[rows, h * HEAD_DIM:(h + 1) * HEAD_DIM] = ((g * jax.nn.sigmoid(g)) * y).astype(BF16)
        yield

    cb, sb = cb_a_ref[rows, :], sb_a_ref[rows, :]
    co, so = row(off, OFF_COS_A), row(off, OFF_SIN_A)
    cos_a = cb * co - sb * so
    sin_a = sb * co + cb * so
    sin_lo = sin_a * row(lane, LANE_MASK_LO)
    sin_hi = sin_a * row(lane, LANE_MASK_HI)
    half = ROPE_DIM // 2

    def rot_a(t):
        return (t * cos_a + pltpu.roll(t, HEAD_DIM - half, 1) * sin_lo
                + pltpu.roll(t, half, 1) * sin_hi)

    qi = lax.broadcasted_iota(jnp.int32, (WINDOW, 2 * WINDOW), 0)
    kj = lax.broadcasted_iota(jnp.int32, (WINDOW, 2 * WINDOW), 1)
    rel = WINDOW + qi - kj
    mask = (rel >= 0) & (rel < WINDOW) & ((kj >= WINDOW) | not_first)
    neg = jnp.finfo(F32).min

    for kh in range(ATT_KV_HEADS):
        kb = rot_a(_rms_rows(zhead("k_a", kh), kn_ref[...])).astype(BF16)
        vb = zhead("v_a", kh)
        kk = jnp.concatenate([kprev_ref[kh], kb], axis=0)
        vv = jnp.concatenate([vprev_ref[kh], vb], axis=0)
        heads = range(kh * GQA_GROUP, (kh + 1) * GQA_GROUP)
        qs = jnp.concatenate(
            [rot_a(_rms_rows(zhead("q_a", qh), qn_ref[...])).astype(BF16) for qh in heads], axis=0)
        s_all = _dot_nt(qs, kk) * (HEAD_DIM ** -0.5)
        probs, inv_denoms = [], []
        for gq, qh in enumerate(heads):
            s = jnp.where(mask, s_all[gq * WINDOW:(gq + 1) * WINDOW, :], neg)
            sink = sink_ref[qh]
            mx = jnp.maximum(jnp.max(s, axis=-1, keepdims=True), sink)
            p = jnp.exp(s - mx)
            denom = jnp.sum(p, axis=-1, keepdims=True) + jnp.exp(sink - mx)
            probs.append(p.astype(BF16))
            inv_denoms.append(1.0 / denom)
        o_all = jnp.dot(jnp.concatenate(probs, axis=0), vv, preferred_element_type=F32)
        for gq, qh in enumerate(heads):
            o = o_all[gq * WINDOW:(gq + 1) * WINDOW, :] * inv_denoms[gq]
            col = RET_WIDTH + qh * HEAD_DIM
            y_ref[rows, col:col + HEAD_DIM] = o.astype(BF16)
        kprev_ref[kh] = kb
        vprev_ref[kh] = vb
        yield


def _load_weights_as_bf16(win_hbm, wout_hbm, win_ref, wout_ref, stage_refs, sem):
    rows = stage_refs[0].shape[0]
    slot_w = wout_hbm.shape[1]
    slabs = [(win_hbm, win_ref, c0, min(slot_w, win_hbm.shape[1] - c0))
             for c0 in range(0, win_hbm.shape[1], slot_w)]
    slabs.append((wout_hbm, wout_ref, 0, slot_w))
    n_slabs = len(slabs)
    slots = [(ref, k * slot_w) for ref in stage_refs for k in range(ref.shape[1] // slot_w)]
    assert len(slots) >= n_slabs and win_hbm.shape[0] == wout_hbm.shape[0]
    n_blocks = win_hbm.shape[0] // rows

    def copy(r, p):
        src, _, c0, width = slabs[p]
        stage, lo = slots[p]
        r0 = pl.multiple_of(r * rows, rows)
        return pltpu.make_async_copy(src.at[pl.ds(r0, rows), pl.ds(c0, width)],
                                     stage.at[:, pl.ds(lo, width)], sem.at[p])

    for p in range(n_slabs - 1):
        copy(0, p).start()

    def body(r, carry):
        for p, (_, dst, c0, width) in enumerate(slabs):
            if p == 0:
                copy(r, n_slabs - 1).start()
            else:
                @pl.when(r + 1 < n_blocks)
                def _():
                    copy(r + 1, p - 1).start()
            copy(r, p).wait()
            stage, lo = slots[p]
            r0 = pl.multiple_of(r * rows, rows)
            dst[pl.ds(r0, rows), c0:c0 + width] = stage[:, lo:lo + width].astype(BF16)
        return carry

    lax.fori_loop(0, n_blocks, body, 0)


def _mix_layer_kernel(sink_ref, gam_ref, xn_ref, gain_ref, win_hbm, cb_r_ref, sb_r_ref,
                      cb_a_ref, sb_a_ref, off_ref, lane_ref, qn_ref, kn_ref, wout_hbm, o_ref,
                      zfa_ref, zva_ref, zfb_ref, zvb_ref, h_ref, xprev_ref, y_ref, state_ref,
                      kprev_ref, vprev_ref, win_ref, wout_ref, wsem, *, proj_tile):
    s = pl.program_id(0)
    rb = xn_ref.shape[0]

    @pl.when(s == 0)
    def _():
        _load_weights_as_bf16(win_hbm, wout_hbm, win_ref, wout_ref, (zfa_ref, zfb_ref), wsem)
        zfb_ref[...] = jnp.zeros_like(zfb_ref)
        zvb_ref[...] = jnp.zeros_like(zvb_ref)
        xprev_ref[...] = jnp.zeros_like(xprev_ref)

    @pl.when(s <= 1)
    def _():
        state_ref[...] = jnp.zeros_like(state_ref)
        kprev_ref[...] = jnp.zeros_like(kprev_ref)
        vprev_ref[...] = jnp.zeros_like(vprev_ref)

    base_refs = (cb_r_ref, sb_r_ref, cb_a_ref, sb_a_ref)

    def step(zw, zr):
        off = off_ref[0]
        lane = lane_ref[...]

        def chunk_pieces(cc):
            rows = slice(cc * CHUNK, (cc + 1) * CHUNK)

            def zhead(name, hd):
                is_value, z_off, _ = _Z_LAYOUT[name]
                return zr[is_value][rows, z_off + hd * HEAD_DIM: z_off + (hd + 1) * HEAD_DIM]

            not_first = (s > 1) if cc == 0 else True
            return _mix_chunk(zhead, rows, not_first, sink_ref, gam_ref, base_refs, off, lane,
                              qn_ref, kn_ref, y_ref, state_ref, kprev_ref, vprev_ref)

        pieces = itertools.chain(*[chunk_pieces(cc) for cc in range(rb // CHUNK)])

        h_ref[...] = _rms_rows(xn_ref[...], gain_ref[...]).astype(BF16)
        for t in range(IN_WIDTH // proj_tile):
            w_col = t * proj_tile
            is_value, z_col = _z_dest(w_col)
            dst = zw[is_value]
            tile = jnp.dot(h_ref[...], win_ref[:, w_col:w_col + proj_tile],
                           preferred_element_type=F32)
            dst[:, z_col:z_col + proj_tile] = tile.astype(dst.dtype)
            next(pieces, None)
        for _ in pieces:
            pass

        o_ref[...] = xprev_ref[...] + jnp.dot(y_ref[...], wout_ref[...],
                                              preferred_element_type=F32)
        xprev_ref[...] = xn_ref[...]

    buf_a = {False: zfa_ref, True: zva_ref}
    buf_b = {False: zfb_ref, True: zvb_ref}

    @pl.when(s % 2 == 0)
    def _():
        step(buf_a, buf_b)

    @pl.when(s % 2 == 1)
    def _():
        step(buf_b, buf_a)


def _mix_layer(x, gain, w_in, w_out, sinks, q_gain, k_gain, *, rb=256, proj_tile=256):
    s, d = x.shape
    nb = s // rb
    t = _position_tables(rb, nb)
    cur = lambda i: (jnp.minimum(i, nb - 1), 0)
    prv = lambda i: (jnp.maximum(i - 1, 0), 0)
    const2 = lambda i: (0, 0)
    resident = dict(pipeline_mode=pl.Buffered(1))
    base_tab = pl.BlockSpec((rb, HEAD_DIM), const2, **resident)
    smem = pl.BlockSpec(memory_space=pltpu.SMEM)
    return pl.pallas_call(
        functools.partial(_mix_layer_kernel, proj_tile=proj_tile),
        name="mix_layer",
        grid=(nb + 1,),
        in_specs=[
            smem, smem,
            pl.BlockSpec((rb, d), cur),
            pl.BlockSpec((1, d), const2),
            pl.BlockSpec(memory_space=pl.ANY),
            base_tab, base_tab, base_tab, base_tab,
            pl.BlockSpec((1,) + t["block_off"].shape[1:], lambda i: (jnp.maximum(i - 1, 0), 0, 0)),
            pl.BlockSpec(t["lane"].shape, const2),
            pl.BlockSpec((1, HEAD_DIM), const2), pl.BlockSpec((1, HEAD_DIM), const2),
            pl.BlockSpec(memory_space=pl.ANY),
        ],
        out_specs=pl.BlockSpec((rb, d), prv),
        out_shape=jax.ShapeDtypeStruct((s, d), F32),
        scratch_shapes=[
            pltpu.VMEM((rb, ZF_WIDTH), F32), pltpu.VMEM((rb, ZV_WIDTH), BF16),
            pltpu.VMEM((rb, ZF_WIDTH), F32), pltpu.VMEM((rb, ZV_WIDTH), BF16),
            pltpu.VMEM((rb, d), BF16),
            pltpu.VMEM((rb, d), F32),
            pltpu.VMEM((rb, MIX_WIDTH), BF16),
            pltpu.VMEM((RET_HEADS, HEAD_DIM, HEAD_DIM), F32),
            pltpu.VMEM((ATT_KV_HEADS, WINDOW, HEAD_DIM), BF16),
            pltpu.VMEM((ATT_KV_HEADS, WINDOW, HEAD_DIM), BF16),
            pltpu.VMEM((d, IN_WIDTH), BF16),
            pltpu.VMEM((MIX_WIDTH, d), BF16),
            pltpu.SemaphoreType.DMA((4,)),
        ],
        compiler_params=pltpu.CompilerParams(
            dimension_semantics=("arbitrary",),
            vmem_limit_bytes=VMEM_LIMIT_BYTES,
        ),
    )(sinks, t["gamma"], x, gain, w_in, t["cos_base_r"], t["sin_base_r"], t["cos_base_a"],
      t["sin_base_a"], t["block_off"], t["lane"], q_gain, k_gain, w_out)


def _position_tables(rb, nb):
    f32 = lambda a: jnp.asarray(np.asarray(a, np.float32))
    r = np.arange(rb, dtype=np.float64)[:, None]
    start = (np.arange(nb, dtype=np.float64) * rb)[:, None]

    inv_r = RET_THETA ** (-np.arange(0, HEAD_DIM, 2, dtype=np.float64) / HEAD_DIM)
    inv_r = np.concatenate([inv_r, inv_r])
    inv_a = ROPE_THETA ** (-np.arange(0, ROPE_DIM, 2, dtype=np.float64) / ROPE_DIM)
    inv_a = np.concatenate([inv_a, inv_a, np.zeros(HEAD_DIM - ROPE_DIM)])

    block_off = np.stack([np.cos(start * inv_r), np.sin(start * inv_r),
                          np.cos(start * inv_a), np.sin(start * inv_a)], axis=1)

    half = ROPE_DIM // 2
    lane = np.zeros((8, HEAD_DIM))
    lane[LANE_SIGN_R] = np.where(np.arange(HEAD_DIM) < HEAD_DIM // 2, -1.0, 1.0)
    lane[LANE_MASK_LO, :half] = -1.0
    lane[LANE_MASK_HI, half:ROPE_DIM] = 1.0

    log_g = np.log1p(-np.exp2(-5.0 - np.arange(RET_HEADS, dtype=np.float64)))
    gamma = np.stack([log_g, np.exp(CHUNK * log_g)])
    return {
        "cos_base_r": f32(np.cos(r * inv_r)), "sin_base_r": f32(np.sin(r * inv_r)),
        "cos_base_a": f32(np.cos(r * inv_a)), "sin_base_a": f32(np.sin(r * inv_a)),
        "block_off": f32(block_off), "lane": f32(lane), "gamma": f32(gamma),
    }


def kernel(x, ffn1_norm, ffn1_w_gate, ffn1_w_up, ffn1_w_down, mix_norm, w_in, q_norm, k_norm,
           attn_sinks, w_out, ffn2_norm, ffn2_w_gate, ffn2_w_up, ffn2_w_down):
    b, s, d = x.shape
    depth = ffn1_norm.shape[0]
    outs = []
    for bi in range(b):
        xb = x[bi]
        for l in range(depth):
            xb = _ffn(xb, ffn1_norm[l][None], ffn1_w_gate[l], ffn1_w_up[l], ffn1_w_down[l])
            xb = _mix_layer(xb, mix_norm[l][None], w_in[l], w_out[l], attn_sinks[l],
                            q_norm[l][None], k_norm[l][None])
            xb = _ffn(xb, ffn2_norm[l][None], ffn2_w_gate[l], ffn2_w_up[l], ffn2_w_down[l])
        outs.append(xb)
    return jnp.stack(outs, axis=0)
```

```python
import functools
import itertools

import jax
import jax.numpy as jnp
import numpy as np
from jax import lax
from jax.experimental import pallas as pl
from jax.experimental.pallas import tpu as pltpu

D_MODEL = 2048
HEAD_DIM = 128
RET_HEADS = 8
ATT_Q_HEADS = 8
ATT_KV_HEADS = 2
GQA_GROUP = ATT_Q_HEADS // ATT_KV_HEADS
RET_WIDTH = RET_HEADS * HEAD_DIM
ATT_WIDTH = ATT_Q_HEADS * HEAD_DIM
KV_WIDTH = ATT_KV_HEADS * HEAD_DIM
MIX_WIDTH = RET_WIDTH + ATT_WIDTH
IN_WIDTH = 4 * RET_WIDTH + ATT_WIDTH + 2 * KV_WIDTH
CHUNK = 128
WINDOW = 128
ROPE_THETA = 500000.0
ROPE_DIM = HEAD_DIM // 4
RET_THETA = 10000.0
EPS = 1e-6

VMEM_LIMIT_BYTES = 62 * 1024 * 1024

F32 = jnp.float32
BF16 = jnp.bfloat16


def _rms_rows(x, gain):
    ms = jnp.mean(x * x, axis=-1, keepdims=True)
    return x * lax.rsqrt(ms + EPS) * gain


def _ffn_kernel(x_hbm, gain_ref, wg_ref, wu_ref, wd_ref, o_ref, h_ref, xbuf_ref, sem,
                *, row_chunk):
    i = pl.program_id(0)
    j = pl.program_id(1)
    n_tiles = pl.num_programs(0)
    tm = o_ref.shape[0]

    def x_copy(tile):
        return pltpu.make_async_copy(x_hbm.at[pl.ds(tile * tm, tm), :], xbuf_ref, sem)

    @pl.when((i == 0) & (j == 0))
    def _():
        x_copy(0).start()

    @pl.when(j == 0)
    def _():
        x_copy(i).wait()

        def body(r, carry):
            rows = pl.ds(pl.multiple_of(r * row_chunk, row_chunk), row_chunk)
            x = xbuf_ref[rows, :]
            h_ref[rows, :] = _rms_rows(x, gain_ref[...]).astype(BF16)
            o_ref[rows, :] = x
            return carry

        lax.fori_loop(0, tm // row_chunk, body, 0)

    @pl.when((j == 1) & (i + 1 < n_tiles))
    def _():
        x_copy(i + 1).start()

    h = h_ref[...]
    g = jnp.dot(h, wg_ref[...].astype(BF16), preferred_element_type=F32)
    u = jnp.dot(h, wu_ref[...].astype(BF16), preferred_element_type=F32)
    a = (0.5 * (g * jax.nn.sigmoid(g)) * u).astype(BF16)
    o_ref[...] += jnp.dot(a, wd_ref[...].astype(BF16), preferred_element_type=F32)


def _ffn(x, gain, wg, wu, wd, *, tm=1024, tf=512, row_chunk=128):
    s, d = x.shape
    d_ff = wg.shape[1]
    return pl.pallas_call(
        functools.partial(_ffn_kernel, row_chunk=row_chunk),
        name="ffn",
        grid=(s // tm, d_ff // tf),
        in_specs=[
            pl.BlockSpec(memory_space=pl.ANY),
            pl.BlockSpec((1, d), lambda i, j: (0, 0)),
            pl.BlockSpec((d, tf), lambda i, j: (0, j)),
            pl.BlockSpec((d, tf), lambda i, j: (0, j)),
            pl.BlockSpec((tf, d), lambda i, j: (j, 0)),
        ],
        out_specs=pl.BlockSpec((tm, d), lambda i, j: (i, 0)),
        out_shape=jax.ShapeDtypeStruct((s, d), F32),
        scratch_shapes=[pltpu.VMEM((tm, d), BF16), pltpu.VMEM((tm, d), F32),
                        pltpu.SemaphoreType.DMA(())],
        compiler_params=pltpu.CompilerParams(
            dimension_semantics=("arbitrary", "arbitrary"),
            vmem_limit_bytes=VMEM_LIMIT_BYTES,
        ),
    )(x, gain, wg, wu, wd)


_SEGMENTS = (("q_r", RET_WIDTH, False), ("k_r", RET_WIDTH, False), ("v_r", RET_WIDTH, True),
             ("g_r", RET_WIDTH, False), ("q_a", ATT_WIDTH, False), ("k_a", KV_WIDTH, False),
             ("v_a", KV_WIDTH, True))


def _z_layout():
    layout, w_off, widths = {}, 0, {False: 0, True: 0}
    for name, width, is_value in _SEGMENTS:
        layout[name] = (is_value, widths[is_value], w_off)
        widths[is_value] += width
        w_off += width
    return layout, widths[False], widths[True]


_Z_LAYOUT, ZF_WIDTH, ZV_WIDTH = _z_layout()


def _z_dest(w_col):
    for name, width, _ in _SEGMENTS:
        is_value, z_off, w_off = _Z_LAYOUT[name]
        if w_off <= w_col < w_off + width:
            return is_value, z_off + (w_col - w_off)
    raise ValueError(w_col)


def _dot_nt(a, b):
    return lax.dot_general(a, b, (((1,), (1,)), ((), ())), preferred_element_type=F32)


def _dot_tn(a, b):
    return lax.dot_general(a, b, (((0,), (0,)), ((), ())), preferred_element_type=F32)


LANE_SIGN_R, LANE_MASK_LO, LANE_MASK_HI = 0, 1, 2
OFF_COS_R, OFF_SIN_R, OFF_COS_A, OFF_SIN_A = 0, 1, 2, 3
GAM_LOG, GAM_CHUNK = 0, 1


def _mix_chunk(zhead, rows, not_first, sink_ref, gam_ref, base_refs, off, lane, qn_ref, kn_ref,
               y_ref, state_ref, kprev_ref, vprev_ref):
    cb_r_ref, sb_r_ref, cb_a_ref, sb_a_ref = base_refs
    row = lambda t, k: t[k:k + 1, :]

    cb, sb = cb_r_ref[rows, :], sb_r_ref[rows, :]
    co, so = row(off, OFF_COS_R), row(off, OFF_SIN_R)
    cos_r = cb * co - sb * so
    sin_r = (sb * co + cb * so) * row(lane, LANE_SIGN_R)

    def rot_r(t):
        return t * cos_r + pltpu.roll(t, HEAD_DIM // 2, 1) * sin_r

    n = lax.broadcasted_iota(jnp.int32, (CHUNK, CHUNK), 0)
    m = lax.broadcasted_iota(jnp.int32, (CHUNK, CHUNK), 1)
    causal = n >= m
    lag = jnp.maximum(n - m, 0).astype(F32)
    n_plus_1 = (n + 1).astype(F32)
    to_end = (CHUNK - 1 - n).astype(F32)

    for h in range(RET_HEADS):
        log_g = gam_ref[GAM_LOG, h]
        q = rot_r(zhead("q_r", h))
        k = rot_r(zhead("k_r", h)) * (HEAD_DIM ** -0.5)
        vb = zhead("v_r", h)
        decay = jnp.where(causal, jnp.exp(lag * log_g), 0.0)
        scores = _dot_nt(q.astype(BF16), k.astype(BF16)) * decay
        prev = state_ref[h]
        q_cross = (q * jnp.exp(n_plus_1 * log_g)).astype(BF16)
        lhs = jnp.concatenate([scores.astype(BF16), q_cross], axis=1)
        rhs = jnp.concatenate([vb, prev.astype(BF16)], axis=0)
        y = jnp.dot(lhs, rhs, preferred_element_type=F32)
        chunk_kv = _dot_tn((k * jnp.exp(to_end * log_g)).astype(BF16), vb)
        state_ref[h] = prev * gam_ref[GAM_CHUNK, h] + chunk_kv
        y = y * lax.rsqrt(jnp.mean(y * y, axis=-1, keepdims=True) + EPS)
        g = zhead("g_r", h)
        y_ref[rows, h * HEAD_DIM:(h + 1) * HEAD_DIM] = ((g * jax.nn.sigmoid(g)) * y).astype(BF16)
        yield

    cb, sb = cb_a_ref[rows, :], sb_a_ref[rows, :]
    co, so = row(off, OFF_COS_A), row(off, OFF_SIN_A)
    cos_a = cb * co - sb * so
    sin_a = sb * co + cb * so
    sin_lo = sin_a * row(lane, LANE_MASK_LO)
    sin_hi = sin_a * row(lane, LANE_MASK_HI)
    half = ROPE_DIM // 2

    def rot_a(t):
        return (t * cos_a + pltpu.roll(t, HEAD_DIM - half, 1) * sin_lo
                + pltpu.roll(t, half, 1) * sin_hi)

    qi = lax.broadcasted_iota(jnp.int32, (WINDOW, 2 * WINDOW), 0)
    kj = lax.broadcasted_iota(jnp.int32, (WINDOW, 2 * WINDOW), 1)
    rel = WINDOW + qi - kj
    mask = (rel >= 0) & (rel < WINDOW) & ((kj >= WINDOW) | not_first)
    neg = jnp.finfo(F32).min

    for kh in range(ATT_KV_HEADS):
        kb = rot_a(_rms_rows(zhead("k_a", kh), kn_ref[...])).astype(BF16)
        vb = zhead("v_a", kh)
        kk = jnp.concatenate([kprev_ref[kh], kb], axis=0)
        vv = jnp.concatenate([vprev_ref[kh], vb], axis=0)
        heads = range(kh * GQA_GROUP, (kh + 1) * GQA_GROUP)
        qs = jnp.concatenate(
            [rot_a(_rms_rows(zhead("q_a", qh), qn_ref[...])).astype(BF16) for qh in heads], axis=0)
        s_all = _dot_nt(qs, kk) * (HEAD_DIM ** -0.5)
        probs, inv_denoms = [], []
        for gq, qh in enumerate(heads):
            s = jnp.where(mask, s_all[gq * WINDOW:(gq + 1) * WINDOW, :], neg)
            sink = sink_ref[qh]
            mx = jnp.maximum(jnp.max(s, axis=-1, keepdims=True), sink)
            p = jnp.exp(s - mx)
            denom = jnp.sum(p, axis=-1, keepdims=True) + jnp.exp(sink - mx)
            probs.append(p.astype(BF16))
            inv_denoms.append(1.0 / denom)
        o_all = jnp.dot(jnp.concatenate(probs, axis=0), vv, preferred_element_type=F32)
        for gq, qh in enumerate(heads):
            o = o_all[gq * WINDOW:(gq + 1) * WINDOW, :] * inv_denoms[gq]
            col = RET_WIDTH + qh * HEAD_DIM
            y_ref[rows, col:col + HEAD_DIM] = o.astype(BF16)
        kprev_ref[kh] = kb
        vprev_ref[kh] = vb
        yield


def _load_weight_as_bf16(w_hbm, w_ref, stage_ref, sem):
    rows = stage_ref.shape[0]
    cols = w_hbm.shape[1]
    n_slabs = next(n for n in range(2, cols // HEAD_DIM + 1, 2)
                   if cols % (n * HEAD_DIM) == 0 and cols // n <= stage_ref.shape[1] // 2)
    slot_w = cols // n_slabs
    slabs = [(p * slot_w, slot_w) for p in range(n_slabs)]
    n_blocks = w_hbm.shape[0] // rows

    def copy(r, p):
        c0, width = slabs[p]
        r0 = pl.multiple_of(r * rows, rows)
        return pltpu.make_async_copy(
            w_hbm.at[pl.ds(r0, rows), pl.ds(c0, width)],
            stage_ref.at[:, pl.ds((p % 2) * slot_w, width)],
            sem.at[p % 2])

    copy(0, 0).start()

    def body(r, carry):
        for p, (c0, width) in enumerate(slabs):
            if p + 1 < len(slabs):
                copy(r, p + 1).start()
            else:
                @pl.when(r + 1 < n_blocks)
                def _():
                    copy(r + 1, 0).start()
            copy(r, p).wait()
            r0 = pl.multiple_of(r * rows, rows)
            lo = (p % 2) * slot_w
            w_ref[pl.ds(r0, rows), c0:c0 + width] = stage_ref[:, lo:lo + width].astype(BF16)
        return carry

    lax.fori_loop(0, n_blocks, body, 0)


def _mix_kernel(sink_ref, gam_ref, xn_ref, gain_ref, win_hbm, cb_r_ref, sb_r_ref, cb_a_ref,
                sb_a_ref, off_ref, lane_ref, qn_ref, kn_ref, y_ref,
                zfa_ref, zva_ref, zfb_ref, zvb_ref, h_ref, state_ref, kprev_ref, vprev_ref,
                win_ref, wsem, *, proj_tile):
    s = pl.program_id(0)
    rb = xn_ref.shape[0]

    @pl.when(s == 0)
    def _():
        _load_weight_as_bf16(win_hbm, win_ref, zfb_ref, wsem)
        zfb_ref[...] = jnp.zeros_like(zfb_ref)
        zvb_ref[...] = jnp.zeros_like(zvb_ref)

    @pl.when(s <= 1)
    def _():
        state_ref[...] = jnp.zeros_like(state_ref)
        kprev_ref[...] = jnp.zeros_like(kprev_ref)
        vprev_ref[...] = jnp.zeros_like(vprev_ref)

    base_refs = (cb_r_ref, sb_r_ref, cb_a_ref, sb_a_ref)

    def step(zw, zr):
        off = off_ref[0]
        lane = lane_ref[...]

        def chunk_pieces(cc):
            rows = slice(cc * CHUNK, (cc + 1) * CHUNK)

            def zhead(name, hd):
                is_value, z_off, _ = _Z_LAYOUT[name]
                return zr[is_value][rows, z_off + hd * HEAD_DIM: z_off + (hd + 1) * HEAD_DIM]

            not_first = (s > 1) if cc == 0 else True
            return _mix_chunk(zhead, rows, not_first, sink_ref, gam_ref, base_refs, off, lane,
                              qn_ref, kn_ref, y_ref, state_ref, kprev_ref, vprev_ref)

        pieces = itertools.chain(*[chunk_pieces(cc) for cc in range(rb // CHUNK)])

        h_ref[...] = _rms_rows(xn_ref[...], gain_ref[...]).astype(BF16)
        for t in range(IN_WIDTH // proj_tile):
            w_col = t * proj_tile
            is_value, z_col = _z_dest(w_col)
            dst = zw[is_value]
            tile = jnp.dot(h_ref[...], win_ref[:, w_col:w_col + proj_tile],
                           preferred_element_type=F32)
            dst[:, z_col:z_col + proj_tile] = tile.astype(dst.dtype)
            next(pieces, None)
        for _ in pieces:
            pass

    buf_a = {False: zfa_ref, True: zva_ref}
    buf_b = {False: zfb_ref, True: zvb_ref}

    @pl.when(s % 2 == 0)
    def _():
        step(buf_a, buf_b)

    @pl.when(s % 2 == 1)
    def _():
        step(buf_b, buf_a)


def _mix(x, gain, w_in, sinks, q_gain, k_gain, *, rb=256, proj_tile=256):
    s, d = x.shape
    nb = s // rb
    t = _position_tables(rb, nb)
    cur = lambda i: (jnp.minimum(i, nb - 1), 0)
    prv = lambda i: (jnp.maximum(i - 1, 0), 0)
    const2 = lambda i: (0, 0)
    resident = dict(pipeline_mode=pl.Buffered(1))
    base_tab = pl.BlockSpec((rb, HEAD_DIM), const2, **resident)
    smem = pl.BlockSpec(memory_space=pltpu.SMEM)
    return pl.pallas_call(
        functools.partial(_mix_kernel, proj_tile=proj_tile),
        name="mix",
        grid=(nb + 1,),
        in_specs=[
            smem, smem,
            pl.BlockSpec((rb, d), cur),
            pl.BlockSpec((1, d), const2),
            pl.BlockSpec(memory_space=pl.ANY),
            base_tab, base_tab, base_tab, base_tab,
            pl.BlockSpec((1,) + t["block_off"].shape[1:], lambda i: (jnp.maximum(i - 1, 0), 0, 0)),
            pl.BlockSpec(t["lane"].shape, const2),
            pl.BlockSpec((1, HEAD_DIM), const2), pl.BlockSpec((1, HEAD_DIM), const2),
        ],
        out_specs=pl.BlockSpec((rb, MIX_WIDTH), prv),
        out_shape=jax.ShapeDtypeStruct((s, MIX_WIDTH), BF16),
        scratch_shapes=[
            pltpu.VMEM((rb, ZF_WIDTH), F32), pltpu.VMEM((rb, ZV_WIDTH), BF16),
            pltpu.VMEM((rb, ZF_WIDTH), F32), pltpu.VMEM((rb, ZV_WIDTH), BF16),
            pltpu.VMEM((rb, d), BF16),
            pltpu.VMEM((RET_HEADS, HEAD_DIM, HEAD_DIM), F32),
            pltpu.VMEM((ATT_KV_HEADS, WINDOW, HEAD_DIM), BF16),
            pltpu.VMEM((ATT_KV_HEADS, WINDOW, HEAD_DIM), BF16),
            pltpu.VMEM((d, IN_WIDTH), BF16),
            pltpu.SemaphoreType.DMA((2,)),
        ],
        compiler_params=pltpu.CompilerParams(
            dimension_semantics=("arbitrary",),
            vmem_limit_bytes=VMEM_LIMIT_BYTES,
        ),
    )(sinks, t["gamma"], x, gain, w_in, t["cos_base_r"], t["sin_base_r"], t["cos_base_a"],
      t["sin_base_a"], t["block_off"], t["lane"], q_gain, k_gain)


def _out_proj_kernel(x_ref, y_ref, w_ref, o_ref):
    o_ref[...] = x_ref[...] + jnp.dot(y_ref[...], w_ref[...].astype(BF16),
                                      preferred_element_type=F32)


def _out_proj(x, y, w, *, tm=512):
    s, d = x.shape
    k = y.shape[1]
    return pl.pallas_call(
        _out_proj_kernel,
        name="out_proj",
        grid=(s // tm,),
        in_specs=[
            pl.BlockSpec((tm, d), lambda i: (i, 0)),
            pl.BlockSpec((tm, k), lambda i: (i, 0)),
            pl.BlockSpec((k, d), lambda i: (0, 0), pipeline_mode=pl.Buffered(1)),
        ],
        out_specs=pl.BlockSpec((tm, d), lambda i: (i, 0)),
        out_shape=jax.ShapeDtypeStruct((s, d), F32),
        compiler_params=pltpu.CompilerParams(
            dimension_semantics=("arbitrary",),
            vmem_limit_bytes=VMEM_LIMIT_BYTES,
        ),
    )(x, y, w)


def _position_tables(rb, nb):
    f32 = lambda a: jnp.asarray(np.asarray(a, np.float32))
    r = np.arange(rb, dtype=np.float64)[:, None]
    start = (np.arange(nb, dtype=np.float64) * rb)[:, None]

    inv_r = RET_THETA ** (-np.arange(0, HEAD_DIM, 2, dtype=np.float64) / HEAD_DIM)
    inv_r = np.concatenate([inv_r, inv_r])
    inv_a = ROPE_THETA ** (-np.arange(0, ROPE_DIM, 2, dtype=np.float64) / ROPE_DIM)
    inv_a = np.concatenate([inv_a, inv_a, np.zeros(HEAD_DIM - ROPE_DIM)])

    block_off = np.stack([np.cos(start * inv_r), np.sin(start * inv_r),
                          np.cos(start * inv_a), np.sin(start * inv_a)], axis=1)

    half = ROPE_DIM // 2
    lane = np.zeros((8, HEAD_DIM))
    lane[LANE_SIGN_R] = np.where(np.arange(HEAD_DIM) < HEAD_DIM // 2, -1.0, 1.0)
    lane[LANE_MASK_LO, :half] = -1.0
    lane[LANE_MASK_HI, half:ROPE_DIM] = 1.0

    log_g = np.log1p(-np.exp2(-5.0 - np.arange(RET_HEADS, dtype=np.float64)))
    gamma = np.stack([log_g, np.exp(CHUNK * log_g)])
    return {
        "cos_base_r": f32(np.cos(r * inv_r)), "sin_base_r": f32(np.sin(r * inv_r)),
        "cos_base_a": f32(np.cos(r * inv_a)), "sin_base_a": f32(np.sin(r * inv_a)),
        "block_off": f32(block_off), "lane": f32(lane), "gamma": f32(gamma),
    }


def kernel(x, ffn1_norm, ffn1_w_gate, ffn1_w_up, ffn1_w_down, mix_norm, w_in, q_norm, k_norm,
           attn_sinks, w_out, ffn2_norm, ffn2_w_gate, ffn2_w_up, ffn2_w_down):
    b, s, d = x.shape
    depth = ffn1_norm.shape[0]
    outs = []
    for bi in range(b):
        xb = x[bi]
        for l in range(depth):
            xb = _ffn(xb, ffn1_norm[l][None], ffn1_w_gate[l], ffn1_w_up[l], ffn1_w_down[l])
            y = _mix(xb, mix_norm[l][None], w_in[l], attn_sinks[l], q_norm[l][None],
                     k_norm[l][None])
            xb = _out_proj(xb, y, w_out[l])
            xb = _ffn(xb, ffn2_norm[l][None], ffn2_w_gate[l], ffn2_w_up[l], ffn2_w_down[l])
        outs.append(xb)
    return jnp.stack(outs, axis=0)
```

```python
import functools
import itertools

import jax
import jax.numpy as jnp
import numpy as np
from jax import lax
from jax.experimental import pallas as pl
from jax.experimental.pallas import tpu as pltpu

D_MODEL = 2048
HEAD_DIM = 128
RET_HEADS = 8
ATT_Q_HEADS = 8
ATT_KV_HEADS = 2
GQA_GROUP = ATT_Q_HEADS // ATT_KV_HEADS
RET_WIDTH = RET_HEADS * HEAD_DIM
ATT_WIDTH = ATT_Q_HEADS * HEAD_DIM
KV_WIDTH = ATT_KV_HEADS * HEAD_DIM
MIX_WIDTH = RET_WIDTH + ATT_WIDTH
IN_WIDTH = 4 * RET_WIDTH + ATT_WIDTH + 2 * KV_WIDTH
CHUNK = 128
WINDOW = 128
ROPE_THETA = 500000.0
ROPE_DIM = HEAD_DIM // 4
RET_THETA = 10000.0
EPS = 1e-6

VMEM_LIMIT_BYTES = 62 * 1024 * 1024

F32 = jnp.float32
BF16 = jnp.bfloat16


def _rms_rows(x, gain):
    ms = jnp.mean(x * x, axis=-1, keepdims=True)
    return x * lax.rsqrt(ms + EPS) * gain


def _ffn_kernel(x_hbm, gain_ref, wg_ref, wu_ref, wd_ref, o_ref, h_ref, xbuf_ref, sem,
                *, row_chunk):
    i = pl.program_id(0)
    j = pl.program_id(1)
    n_tiles = pl.num_programs(0)
    tm = o_ref.shape[0]

    def x_copy(tile):
        return pltpu.make_async_copy(x_hbm.at[pl.ds(tile * tm, tm), :], xbuf_ref, sem)

    @pl.when((i == 0) & (j == 0))
    def _():
        x_copy(0).start()

    @pl.when(j == 0)
    def _():
        x_copy(i).wait()
        for r in range(tm // row_chunk):
            rows = pl.ds(r * row_chunk, row_chunk)
            x = xbuf_ref[rows, :]
            h_ref[rows, :] = _rms_rows(x, gain_ref[...]).astype(BF16)
            o_ref[rows, :] = x

    @pl.when((j == 1) & (i + 1 < n_tiles))
    def _():
        x_copy(i + 1).start()

    h = h_ref[...]
    g = jnp.dot(h, wg_ref[...].astype(BF16), preferred_element_type=F32)
    u = jnp.dot(h, wu_ref[...].astype(BF16), preferred_element_type=F32)
    a = (0.5 * (g * jax.nn.sigmoid(g)) * u).astype(BF16)
    o_ref[...] += jnp.dot(a, wd_ref[...].astype(BF16), preferred_element_type=F32)


def _ffn(x, gain, wg, wu, wd, *, tm=1024, tf=512, row_chunk=128):
    s, d = x.shape
    d_ff = wg.shape[1]
    return pl.pallas_call(
        functools.partial(_ffn_kernel, row_chunk=row_chunk),
        name="ffn",
        grid=(s // tm, d_ff // tf),
        in_specs=[
            pl.BlockSpec(memory_space=pl.ANY),
            pl.BlockSpec((1, d), lambda i, j: (0, 0)),
            pl.BlockSpec((d, tf), lambda i, j: (0, j)),
            pl.BlockSpec((d, tf), lambda i, j: (0, j)),
            pl.BlockSpec((tf, d), lambda i, j: (j, 0)),
        ],
        out_specs=pl.BlockSpec((tm, d), lambda i, j: (i, 0)),
        out_shape=jax.ShapeDtypeStruct((s, d), F32),
        scratch_shapes=[pltpu.VMEM((tm, d), BF16), pltpu.VMEM((tm, d), F32),
                        pltpu.SemaphoreType.DMA(())],
        compiler_params=pltpu.CompilerParams(
            dimension_semantics=("arbitrary", "arbitrary"),
            vmem_limit_bytes=VMEM_LIMIT_BYTES,
        ),
    )(x, gain, wg, wu, wd)


_SEGMENTS = (("q_r", RET_WIDTH, False), ("k_r", RET_WIDTH, False), ("v_r", RET_WIDTH, True),
             ("g_r", RET_WIDTH, False), ("q_a", ATT_WIDTH, False), ("k_a", KV_WIDTH, False),
             ("v_a", KV_WIDTH, True))


def _z_layout():
    layout, w_off, widths = {}, 0, {False: 0, True: 0}
    for name, width, is_value in _SEGMENTS:
        layout[name] = (is_value, widths[is_value], w_off)
        widths[is_value] += width
        w_off += width
    return layout, widths[False], widths[True]


_Z_LAYOUT, ZF_WIDTH, ZV_WIDTH = _z_layout()


def _z_dest(w_col):
    for name, width, _ in _SEGMENTS:
        is_value, z_off, w_off = _Z_LAYOUT[name]
        if w_off <= w_col < w_off + width:
            return is_value, z_off + (w_col - w_off)
    raise ValueError(w_col)


def _dot_nt(a, b):
    return lax.dot_general(a, b, (((1,), (1,)), ((), ())), preferred_element_type=F32)


def _dot_tn(a, b):
    return lax.dot_general(a, b, (((0,), (0,)), ((), ())), preferred_element_type=F32)


LANE_SIGN_R, LANE_MASK_LO, LANE_MASK_HI = 0, 1, 2
OFF_COS_R, OFF_SIN_R, OFF_COS_A, OFF_SIN_A = 0, 1, 2, 3
GAM_LOG, GAM_CHUNK = 0, 1


def _mix_chunk(zhead, rows, not_first, sink_ref, gam_ref, base_refs, off, lane, qn_ref, kn_ref,
               y_ref, state_ref, kprev_ref, vprev_ref):
    cb_r_ref, sb_r_ref, cb_a_ref, sb_a_ref = base_refs
    row = lambda t, k: t[k:k + 1, :]

    cb, sb = cb_r_ref[rows, :], sb_r_ref[rows, :]
    co, so = row(off, OFF_COS_R), row(off, OFF_SIN_R)
    cos_r = cb * co - sb * so
    sin_r = (sb * co + cb * so) * row(lane, LANE_SIGN_R)

    def rot_r(t):
        return t * cos_r + pltpu.roll(t, HEAD_DIM // 2, 1) * sin_r

    n = lax.broadcasted_iota(jnp.int32, (CHUNK, CHUNK), 0)
    m = lax.broadcasted_iota(jnp.int32, (CHUNK, CHUNK), 1)
    causal = n >= m
    lag = jnp.maximum(n - m, 0).astype(F32)
    n_plus_1 = (n + 1).astype(F32)
    to_end = (CHUNK - 1 - n).astype(F32)

    for h in range(RET_HEADS):
        log_g = gam_ref[GAM_LOG, h]
        q = rot_r(zhead("q_r", h))
        k = rot_r(zhead("k_r", h)) * (HEAD_DIM ** -0.5)
        vb = zhead("v_r", h)
        decay = jnp.where(causal, jnp.exp(lag * log_g), 0.0)
        scores = _dot_nt(q.astype(BF16), k.astype(BF16)) * decay
        prev = state_ref[h]
        q_cross = (q * jnp.exp(n_plus_1 * log_g)).astype(BF16)
        lhs = jnp.concatenate([scores.astype(BF16), q_cross], axis=1)
        rhs = jnp.concatenate([vb, prev.astype(BF16)], axis=0)
        y = jnp.dot(lhs, rhs, preferred_element_type=F32)
        chunk_kv = _dot_tn((k * jnp.exp(to_end * log_g)).astype(BF16), vb)
        state_ref[h] = prev * gam_ref[GAM_CHUNK, h] + chunk_kv
        y = y * lax.rsqrt(jnp.mean(y * y, axis=-1, keepdims=True) + EPS)
        g = zhead("g_r", h)
        y_ref[rows, h * HEAD_DIM:(h + 1) * HEAD_DIM] = ((g * jax.nn.sigmoid(g)) * y).astype(BF16)
        yield

    cb, sb = cb_a_ref[rows, :], sb_a_ref[rows, :]
    co, so = row(off, OFF_COS_A), row(off, OFF_SIN_A)
    cos_a = cb * co - sb * so
    sin_a = sb * co + cb * so
    sin_lo = sin_a * row(lane, LANE_MASK_LO)
    sin_hi = sin_a * row(lane, LANE_MASK_HI)
    half = ROPE_DIM // 2

    def rot_a(t):
        return (t * cos_a + pltpu.roll(t, HEAD_DIM - half, 1) * sin_lo
                + pltpu.roll(t, half, 1) * sin_hi)

    qi = lax.broadcasted_iota(jnp.int32, (WINDOW, 2 * WINDOW), 0)
    kj = lax.broadcasted_iota(jnp.int32, (WINDOW, 2 * WINDOW), 1)
    rel = WINDOW + qi - kj
    mask = (rel >= 0) & (rel < WINDOW) & ((kj >= WINDOW) | not_first)
    neg = jnp.finfo(F32).min

    for kh in range(ATT_KV_HEADS):
        kb = rot_a(_rms_rows(zhead("k_a", kh), kn_ref[...])).astype(BF16)
        vb = zhead("v_a", kh)
        kk = jnp.concatenate([kprev_ref[kh], kb], axis=0)
        vv = jnp.concatenate([vprev_ref[kh], vb], axis=0)
        heads = range(kh * GQA_GROUP, (kh + 1) * GQA_GROUP)
        qs = jnp.concatenate(
            [rot_a(_rms_rows(zhead("q_a", qh), qn_ref[...])).astype(BF16) for qh in heads], axis=0)
        s_all = _dot_nt(qs, kk) * (HEAD_DIM ** -0.5)
        probs, inv_denoms = [], []
        for gq, qh in enumerate(heads):
            s = jnp.where(mask, s_all[gq * WINDOW:(gq + 1) * WINDOW, :], neg)
            sink = sink_ref[qh]
            mx = jnp.maximum(jnp.max(s, axis=-1, keepdims=True), sink)
            p = jnp.exp(s - mx)
            denom = jnp.sum(p, axis=-1, keepdims=True) + jnp.exp(sink - mx)
            probs.append(p.astype(BF16))
            inv_denoms.append(1.0 / denom)
        o_all = jnp.dot(jnp.concatenate(probs, axis=0), vv, preferred_element_type=F32)
        for gq, qh in enumerate(heads):
            o = o_all[gq * WINDOW:(gq + 1) * WINDOW, :] * inv_denoms[gq]
            col = RET_WIDTH + qh * HEAD_DIM
            y_ref[rows, col:col + HEAD_DIM] = o.astype(BF16)
        kprev_ref[kh] = kb
        vprev_ref[kh] = vb
        yield


def _load_weights_as_bf16(win_hbm, wout_hbm, win_ref, wout_ref, stage_ref, sem):
    rows = stage_ref.shape[0]
    slot_w = wout_hbm.shape[1]
    assert stage_ref.shape[1] >= 2 * slot_w and win_hbm.shape[0] == wout_hbm.shape[0]
    slabs = [(win_hbm, win_ref, c0, min(slot_w, win_hbm.shape[1] - c0))
             for c0 in range(0, win_hbm.shape[1], slot_w)]
    slabs.append((wout_hbm, wout_ref, 0, slot_w))
    assert len(slabs) % 2 == 0
    n_blocks = win_hbm.shape[0] // rows

    def copy(r, p):
        src, _, c0, width = slabs[p]
        r0 = pl.multiple_of(r * rows, rows)
        return pltpu.make_async_copy(
            src.at[pl.ds(r0, rows), pl.ds(c0, width)],
            stage_ref.at[:, pl.ds((p % 2) * slot_w, width)],
            sem.at[p % 2])

    copy(0, 0).start()

    def body(r, carry):
        for p, (_, dst, c0, width) in enumerate(slabs):
            if p + 1 < len(slabs):
                copy(r, p + 1).start()
            else:
                @pl.when(r + 1 < n_blocks)
                def _():
                    copy(r + 1, 0).start()
            copy(r, p).wait()
            r0 = pl.multiple_of(r * rows, rows)
            lo = (p % 2) * slot_w
            dst[pl.ds(r0, rows), c0:c0 + width] = stage_ref[:, lo:lo + width].astype(BF16)
        return carry

    lax.fori_loop(0, n_blocks, body, 0)


def _mix_layer_kernel(sink_ref, gam_ref, xn_ref, gain_ref, win_hbm, cb_r_ref, sb_r_ref,
                      cb_a_ref, sb_a_ref, off_ref, lane_ref, qn_ref, kn_ref, wout_hbm, o_ref,
                      zfa_ref, zva_ref, zfb_ref, zvb_ref, h_ref, xprev_ref, y_ref, state_ref,
                      kprev_ref, vprev_ref, win_ref, wout_ref, wsem, *, proj_tile):
    s = pl.program_id(0)
    rb = xn_ref.shape[0]

    @pl.when(s == 0)
    def _():
        _load_weights_as_bf16(win_hbm, wout_hbm, win_ref, wout_ref, zfb_ref, wsem)
        zfb_ref[...] = jnp.zeros_like(zfb_ref)
        zvb_ref[...] = jnp.zeros_like(zvb_ref)
        xprev_ref[...] = jnp.zeros_like(xprev_ref)

    @pl.when(s <= 1)
    def _():
        state_ref[...] = jnp.zeros_like(state_ref)
        kprev_ref[...] = jnp.zeros_like(kprev_ref)
        vprev_ref[...] = jnp.zeros_like(vprev_ref)

    base_refs = (cb_r_ref, sb_r_ref, cb_a_ref, sb_a_ref)

    def step(zw, zr):
        off = off_ref[0]
        lane = lane_ref[...]

        def chunk_pieces(cc):
            rows = slice(cc * CHUNK, (cc + 1) * CHUNK)

            def zhead(name, hd):
                is_value, z_off, _ = _Z_LAYOUT[name]
                return zr[is_value][rows, z_off + hd * HEAD_DIM: z_off + (hd + 1) * HEAD_DIM]

            not_first = (s > 1) if cc == 0 else True
            return _mix_chunk(zhead, rows, not_first, sink_ref, gam_ref, base_refs, off, lane,
                              qn_ref, kn_ref, y_ref, state_ref, kprev_ref, vprev_ref)

        pieces = itertools.chain(*[chunk_pieces(cc) for cc in range(rb // CHUNK)])

        h_ref[...] = _rms_rows(xn_ref[...], gain_ref[...]).astype(BF16)
        for t in range(IN_WIDTH // proj_tile):
            w_col = t * proj_tile
            is_value, z_col = _z_dest(w_col)
            dst = zw[is_value]
            tile = jnp.dot(h_ref[...], win_ref[:, w_col:w_col + proj_tile],
                           preferred_element_type=F32)
            dst[:, z_col:z_col + proj_tile] = tile.astype(dst.dtype)
            next(pieces, None)
        for _ in pieces:
            pass

        o_ref[...] = xprev_ref[...] + jnp.dot(y_ref[...], wout_ref[...],
                                              preferred_element_type=F32)
        xprev_ref[...] = xn_ref[...]

    buf_a = {False: zfa_ref, True: zva_ref}
    buf_b = {False: zfb_ref, True: zvb_ref}

    @pl.when(s % 2 == 0)
    def _():
        step(buf_a, buf_b)

    @pl.when(s % 2 == 1)
    def _():
        step(buf_b, buf_a)


def _mix_layer(x, gain, w_in, w_out, sinks, q_gain, k_gain, *, rb=256, proj_tile=256):
    s, d = x.shape
    nb = s // rb
    t = _position_tables(rb, nb)
    cur = lambda i: (jnp.minimum(i, nb - 1), 0)
    prv = lambda i: (jnp.maximum(i - 1, 0), 0)
    const2 = lambda i: (0, 0)
    resident = dict(pipeline_mode=pl.Buffered(1))
    base_tab = pl.BlockSpec((rb, HEAD_DIM), const2, **resident)
    smem = pl.BlockSpec(memory_space=pltpu.SMEM)
    return pl.pallas_call(
        functools.partial(_mix_layer_kernel, proj_tile=proj_tile),
        name="mix_layer",
        grid=(nb + 1,),
        in_specs=[
            smem, smem,
            pl.BlockSpec((rb, d), cur),
            pl.BlockSpec((1, d), const2),
            pl.BlockSpec(memory_space=pl.ANY),
            base_tab, base_tab, base_tab, base_tab,
            pl.BlockSpec((1,) + t["block_off"].shape[1:], lambda i: (jnp.maximum(i - 1, 0), 0, 0)),
            pl.BlockSpec(t["lane"].shape, const2),
            pl.BlockSpec((1, HEAD_DIM), const2), pl.BlockSpec((1, HEAD_DIM), const2),
            pl.BlockSpec(memory_space=pl.ANY),
        ],
        out_specs=pl.BlockSpec((rb, d), prv),
        out_shape=jax.ShapeDtypeStruct((s, d), F32),
        scratch_shapes=[
            pltpu.VMEM((rb, ZF_WIDTH), F32), pltpu.VMEM((rb, ZV_WIDTH), BF16),
            pltpu.VMEM((rb, ZF_WIDTH), F32), pltpu.VMEM((rb, ZV_WIDTH), BF16),
            pltpu.VMEM((rb, d), BF16),
            pltpu.VMEM((rb, d), F32),
            pltpu.VMEM((rb, MIX_WIDTH), BF16),
            pltpu.VMEM((RET_HEADS, HEAD_DIM, HEAD_DIM), F32),
            pltpu.VMEM((ATT_KV_HEADS, WINDOW, HEAD_DIM), BF16),
            pltpu.VMEM((ATT_KV_HEADS, WINDOW, HEAD_DIM), BF16),
            pltpu.VMEM((d, IN_WIDTH), BF16),
            pltpu.VMEM((MIX_WIDTH, d), BF16),
            pltpu.SemaphoreType.DMA((2,)),
        ],
        compiler_params=pltpu.CompilerParams(
            dimension_semantics=("arbitrary",),
            vmem_limit_bytes=VMEM_LIMIT_BYTES,
        ),
    )(sinks, t["gamma"], x, gain, w_in, t["cos_base_r"], t["sin_base_r"], t["cos_base_a"],
      t["sin_base_a"], t["block_off"], t["lane"], q_gain, k_gain, w_out)


def _position_tables(rb, nb):
    f32 = lambda a: jnp.asarray(np.asarray(a, np.float32))
    r = np.arange(rb, dtype=np.float64)[:, None]
    start = (np.arange(nb, dtype=np.float64) * rb)[:, None]

    inv_r = RET_THETA ** (-np.arange(0, HEAD_DIM, 2, dtype=np.float64) / HEAD_DIM)
    inv_r = np.concatenate([inv_r, inv_r])
    inv_a = ROPE_THETA ** (-np.arange(0, ROPE_DIM, 2, dtype=np.float64) / ROPE_DIM)
    inv_a = np.concatenate([inv_a, inv_a, np.zeros(HEAD_DIM - ROPE_DIM)])

    block_off = np.stack([np.cos(start * inv_r), np.sin(start * inv_r),
                          np.cos(start * inv_a), np.sin(start * inv_a)], axis=1)

    half = ROPE_DIM // 2
    lane = np.zeros((8, HEAD_DIM))
    lane[LANE_SIGN_R] = np.where(np.arange(HEAD_DIM) < HEAD_DIM // 2, -1.0, 1.0)
    lane[LANE_MASK_LO, :half] = -1.0
    lane[LANE_MASK_HI, half:ROPE_DIM] = 1.0

    log_g = np.log1p(-np.exp2(-5.0 - np.arange(RET_HEADS, dtype=np.float64)))
    gamma = np.stack([log_g, np.exp(CHUNK * log_g)])
    return {
        "cos_base_r": f32(np.cos(r * inv_r)), "sin_base_r": f32(np.sin(r * inv_r)),
        "cos_base_a": f32(np.cos(r * inv_a)), "sin_base_a": f32(np.sin(r * inv_a)),
        "block_off": f32(block_off), "lane": f32(lane), "gamma": f32(gamma),
    }


def kernel(x, ffn1_norm, ffn1_w_gate, ffn1_w_up, ffn1_w_down, mix_norm, w_in, q_norm, k_norm,
           attn_sinks, w_out, ffn2_norm, ffn2_w_gate, ffn2_w_up, ffn2_w_down):
    b, s, d = x.shape
    depth = ffn1_norm.shape[0]
    outs = []
    for bi in range(b):
        xb = x[bi]
        for l in range(depth):
            xb = _ffn(xb, ffn1_norm[l][None], ffn1_w_gate[l], ffn1_w_up[l], ffn1_w_down[l])
            xb = _mix_layer(xb, mix_norm[l][None], w_in[l], w_out[l], attn_sinks[l],
                            q_norm[l][None], k_norm[l][None])
            xb = _ffn(xb, ffn2_norm[l][None], ffn2_w_gate[l], ffn2_w_up[l], ffn2_w_down[l])
        outs.append(xb)
    return jnp.stack(outs, axis=0)
```

```python
import functools
import itertools

import jax
import jax.numpy as jnp
import numpy as np
from jax import lax
from jax.experimental import pallas as pl
from jax.experimental.pallas import tpu as pltpu

D_MODEL = 2048
HEAD_DIM = 128
RET_HEADS = 8
ATT_Q_HEADS = 8
ATT_KV_HEADS = 2
GQA_GROUP = ATT_Q_HEADS // ATT_KV_HEADS
RET_WIDTH = RET_HEADS * HEAD_DIM
ATT_WIDTH = ATT_Q_HEADS * HEAD_DIM
KV_WIDTH = ATT_KV_HEADS * HEAD_DIM
MIX_WIDTH = RET_WIDTH + ATT_WIDTH
IN_WIDTH = 4 * RET_WIDTH + ATT_WIDTH + 2 * KV_WIDTH
CHUNK = 128
WINDOW = 128
ROPE_THETA = 500000.0
ROPE_DIM = HEAD_DIM // 4
RET_THETA = 10000.0
EPS = 1e-6

VMEM_LIMIT_BYTES = 62 * 1024 * 1024

F32 = jnp.float32
BF16 = jnp.bfloat16


def _rms_rows(x, gain):
    ms = jnp.mean(x * x, axis=-1, keepdims=True)
    return x * lax.rsqrt(ms + EPS) * gain


def _ffn_kernel(*refs, row_chunk, n_cast):
    x_hbm, gain_ref, wg_ref, wu_ref, wd_ref = refs[:5]
    cast_src = refs[5:5 + n_cast]
    o_ref = refs[5 + n_cast]
    cast_dst = refs[6 + n_cast:6 + 2 * n_cast]
    h_ref, xbuf_ref, sem = refs[6 + 2 * n_cast:]
    i = pl.program_id(0)
    j = pl.program_id(1)
    n_tiles = pl.num_programs(0)
    tm = o_ref.shape[0]

    for src, dst in zip(cast_src, cast_dst):
        dst[...] = src[...].astype(BF16)

    def x_copy(tile):
        return pltpu.make_async_copy(x_hbm.at[pl.ds(tile * tm, tm), :], xbuf_ref, sem)

    @pl.when((i == 0) & (j == 0))
    def _():
        x_copy(0).start()

    @pl.when(j == 0)
    def _():
        x_copy(i).wait()
        for r in range(tm // row_chunk):
            rows = pl.ds(r * row_chunk, row_chunk)
            x = xbuf_ref[rows, :]
            h_ref[rows, :] = _rms_rows(x, gain_ref[...]).astype(BF16)
            o_ref[rows, :] = x

    @pl.when((j == 1) & (i + 1 < n_tiles))
    def _():
        x_copy(i + 1).start()

    h = h_ref[...]
    g = jnp.dot(h, wg_ref[...].astype(BF16), preferred_element_type=F32)
    u = jnp.dot(h, wu_ref[...].astype(BF16), preferred_element_type=F32)
    a = (0.5 * (g * jax.nn.sigmoid(g)) * u).astype(BF16)
    o_ref[...] += jnp.dot(a, wd_ref[...].astype(BF16), preferred_element_type=F32)


def _ffn(x, gain, wg, wu, wd, cast=(), *, tm=1024, tf=512, row_chunk=128):
    s, d = x.shape
    d_ff = wg.shape[1]
    n_i, n_j = s // tm, d_ff // tf
    cast_specs = []
    for w in cast:
        n_blocks = max(n for n in range(1, n_i * n_j + 1)
                       if w.shape[0] % n == 0 and (w.shape[0] // n) % 16 == 0)
        index = lambda i, j, n_blocks=n_blocks: (jnp.minimum(i * n_j + j, n_blocks - 1), 0)
        cast_specs.append(pl.BlockSpec((w.shape[0] // n_blocks, w.shape[1]), index))
    outs = pl.pallas_call(
        functools.partial(_ffn_kernel, row_chunk=row_chunk, n_cast=len(cast)),
        name="ffn",
        grid=(n_i, n_j),
        in_specs=[
            pl.BlockSpec(memory_space=pl.ANY),
            pl.BlockSpec((1, d), lambda i, j: (0, 0)),
            pl.BlockSpec((d, tf), lambda i, j: (0, j)),
            pl.BlockSpec((d, tf), lambda i, j: (0, j)),
            pl.BlockSpec((tf, d), lambda i, j: (j, 0)),
            *cast_specs,
        ],
        out_specs=[pl.BlockSpec((tm, d), lambda i, j: (i, 0)), *cast_specs],
        out_shape=[jax.ShapeDtypeStruct((s, d), F32),
                   *[jax.ShapeDtypeStruct(w.shape, BF16) for w in cast]],
        scratch_shapes=[pltpu.VMEM((tm, d), BF16), pltpu.VMEM((tm, d), F32),
                        pltpu.SemaphoreType.DMA(())],
        compiler_params=pltpu.CompilerParams(
            dimension_semantics=("arbitrary", "arbitrary"),
            vmem_limit_bytes=VMEM_LIMIT_BYTES,
        ),
    )(x, gain, wg, wu, wd, *cast)
    return tuple(outs)


_SEGMENTS = (("q_r", RET_WIDTH, False), ("k_r", RET_WIDTH, False), ("v_r", RET_WIDTH, True),
             ("g_r", RET_WIDTH, False), ("q_a", ATT_WIDTH, False), ("k_a", KV_WIDTH, False),
             ("v_a", KV_WIDTH, True))


def _z_layout():
    layout, w_off, widths = {}, 0, {False: 0, True: 0}
    for name, width, is_value in _SEGMENTS:
        layout[name] = (is_value, widths[is_value], w_off)
        widths[is_value] += width
        w_off += width
    return layout, widths[False], widths[True]


_Z_LAYOUT, ZF_WIDTH, ZV_WIDTH = _z_layout()


def _z_dest(w_col):
    for name, width, _ in _SEGMENTS:
        is_value, z_off, w_off = _Z_LAYOUT[name]
        if w_off <= w_col < w_off + width:
            return is_value, z_off + (w_col - w_off)
    raise ValueError(w_col)


def _dot_nt(a, b):
    return lax.dot_general(a, b, (((1,), (1,)), ((), ())), preferred_element_type=F32)


def _dot_tn(a, b):
    return lax.dot_general(a, b, (((0,), (0,)), ((), ())), preferred_element_type=F32)


LANE_SIGN_R, LANE_MASK_LO, LANE_MASK_HI = 0, 1, 2
OFF_COS_R, OFF_SIN_R, OFF_COS_A, OFF_SIN_A = 0, 1, 2, 3
GAM_LOG, GAM_CHUNK = 0, 1


def _mix_chunk(zhead, rows, not_first, sink_ref, gam_ref, base_refs, off, lane, qn_ref, kn_ref,
               y_ref, state_ref, kprev_ref, vprev_ref):
    cb_r_ref, sb_r_ref, cb_a_ref, sb_a_ref = base_refs
    row = lambda t, k: t[k:k + 1, :]

    cb, sb = cb_r_ref[rows, :], sb_r_ref[rows, :]
    co, so = row(off, OFF_COS_R), row(off, OFF_SIN_R)
    cos_r = cb * co - sb * so
    sin_r = (sb * co + cb * so) * row(lane, LANE_SIGN_R)

    def rot_r(t):
        return t * cos_r + pltpu.roll(t, HEAD_DIM // 2, 1) * sin_r

    n = lax.broadcasted_iota(jnp.int32, (CHUNK, CHUNK), 0)
    m = lax.broadcasted_iota(jnp.int32, (CHUNK, CHUNK), 1)
    causal = n >= m
    lag = jnp.maximum(n - m, 0).astype(F32)
    n_plus_1 = (n + 1).astype(F32)
    to_end = (CHUNK - 1 - n).astype(F32)

    for h in range(RET_HEADS):
        log_g = gam_ref[GAM_LOG, h]
        q = rot_r(zhead("q_r", h))
        k = rot_r(zhead("k_r", h)) * (HEAD_DIM ** -0.5)
        vb = zhead("v_r", h)
        decay = jnp.where(causal, jnp.exp(lag * log_g), 0.0)
        scores = _dot_nt(q.astype(BF16), k.astype(BF16)) * decay
        prev = state_ref[h]
        q_cross = (q * jnp.exp(n_plus_1 * log_g)).astype(BF16)
        lhs = jnp.concatenate([scores.astype(BF16), q_cross], axis=1)
        rhs = jnp.concatenate([vb, prev.astype(BF16)], axis=0)
        y = jnp.dot(lhs, rhs, preferred_element_type=F32)
        chunk_kv = _dot_tn((k * jnp.exp(to_end * log_g)).astype(BF16), vb)
        state_ref[h] = prev * gam_ref[GAM_CHUNK, h] + chunk_kv
        y = y * lax.rsqrt(jnp.mean(y * y, axis=-1, keepdims=True) + EPS)
        g = zhead("g_r", h)
        y_ref[rows, h * HEAD_DIM:(h + 1) * HEAD_DIM] = ((g * jax.nn.sigmoid(g)) * y).astype(BF16)
        yield

    cb, sb = cb_a_ref[rows, :], sb_a_ref[rows, :]
    co, so = row(off, OFF_COS_A), row(off, OFF_SIN_A)
    cos_a = cb * co - sb * so
    sin_a = sb * co + cb * so
    sin_lo = sin_a * row(lane, LANE_MASK_LO)
    sin_hi = sin_a * row(lane, LANE_MASK_HI)
    half = ROPE_DIM // 2

    def rot_a(t):
        return (t * cos_a + pltpu.roll(t, HEAD_DIM - half, 1) * sin_lo
                + pltpu.roll(t, half, 1) * sin_hi)

    qi = lax.broadcasted_iota(jnp.int32, (WINDOW, 2 * WINDOW), 0)
    kj = lax.broadcasted_iota(jnp.int32, (WINDOW, 2 * WINDOW), 1)
    rel = WINDOW + qi - kj
    mask = (rel >= 0) & (rel < WINDOW) & ((kj >= WINDOW) | not_first)
    neg = jnp.finfo(F32).min

    for kh in range(ATT_KV_HEADS):
        kb = rot_a(_rms_rows(zhead("k_a", kh), kn_ref[...])).astype(BF16)
        vb = zhead("v_a", kh)
        kk = jnp.concatenate([kprev_ref[kh], kb], axis=0)
        vv = jnp.concatenate([vprev_ref[kh], vb], axis=0)
        heads = range(kh * GQA_GROUP, (kh + 1) * GQA_GROUP)
        qs = jnp.concatenate(
            [rot_a(_rms_rows(zhead("q_a", qh), qn_ref[...])).astype(BF16) for qh in heads], axis=0)
        s_all = _dot_nt(qs, kk) * (HEAD_DIM ** -0.5)
        probs, inv_denoms = [], []
        for gq, qh in enumerate(heads):
            s = jnp.where(mask, s_all[gq * WINDOW:(gq + 1) * WINDOW, :], neg)
            sink = sink_ref[qh]
            mx = jnp.maximum(jnp.max(s, axis=-1, keepdims=True), sink)
            p = jnp.exp(s - mx)
            denom = jnp.sum(p, axis=-1, keepdims=True) + jnp.exp(sink - mx)
            probs.append(p.astype(BF16))
            inv_denoms.append(1.0 / denom)
        o_all = jnp.dot(jnp.concatenate(probs, axis=0), vv, preferred_element_type=F32)
        for gq, qh in enumerate(heads):
            o = o_all[gq * WINDOW:(gq + 1) * WINDOW, :] * inv_denoms[gq]
            col = RET_WIDTH + qh * HEAD_DIM
            y_ref[rows, col:col + HEAD_DIM] = o.astype(BF16)
        kprev_ref[kh] = kb
        vprev_ref[kh] = vb
        yield


def _mix_layer_kernel(sink_ref, gam_ref, xn_ref, gain_ref, win_ref, cb_r_ref, sb_r_ref,
                      cb_a_ref, sb_a_ref, off_ref, lane_ref, qn_ref, kn_ref, wout_ref, o_ref,
                      zfa_ref, zva_ref, zfb_ref, zvb_ref, h_ref, xprev_ref, y_ref, state_ref,
                      kprev_ref, vprev_ref, *, proj_tile):
    s = pl.program_id(0)
    rb = xn_ref.shape[0]

    @pl.when(s == 0)
    def _():
        zfb_ref[...] = jnp.zeros_like(zfb_ref)
        zvb_ref[...] = jnp.zeros_like(zvb_ref)
        xprev_ref[...] = jnp.zeros_like(xprev_ref)

    @pl.when(s <= 1)
    def _():
        state_ref[...] = jnp.zeros_like(state_ref)
        kprev_ref[...] = jnp.zeros_like(kprev_ref)
        vprev_ref[...] = jnp.zeros_like(vprev_ref)

    base_refs = (cb_r_ref, sb_r_ref, cb_a_ref, sb_a_ref)

    def step(zw, zr):
        off = off_ref[0]
        lane = lane_ref[...]

        def chunk_pieces(cc):
            rows = slice(cc * CHUNK, (cc + 1) * CHUNK)

            def zhead(name, hd):
                is_value, z_off, _ = _Z_LAYOUT[name]
                return zr[is_value][rows, z_off + hd * HEAD_DIM: z_off + (hd + 1) * HEAD_DIM]

            not_first = (s > 1) if cc == 0 else True
            return _mix_chunk(zhead, rows, not_first, sink_ref, gam_ref, base_refs, off, lane,
                              qn_ref, kn_ref, y_ref, state_ref, kprev_ref, vprev_ref)

        pieces = itertools.chain(*[chunk_pieces(cc) for cc in range(rb // CHUNK)])

        h_ref[...] = _rms_rows(xn_ref[...], gain_ref[...]).astype(BF16)
        for t in range(IN_WIDTH // proj_tile):
            w_col = t * proj_tile
            is_value, z_col = _z_dest(w_col)
            dst = zw[is_value]
            tile = jnp.dot(h_ref[...], win_ref[:, w_col:w_col + proj_tile],
                           preferred_element_type=F32)
            dst[:, z_col:z_col + proj_tile] = tile.astype(dst.dtype)
            next(pieces, None)
        for _ in pieces:
            pass

        o_ref[...] = xprev_ref[...] + jnp.dot(y_ref[...], wout_ref[...],
                                              preferred_element_type=F32)
        xprev_ref[...] = xn_ref[...]

    buf_a = {False: zfa_ref, True: zva_ref}
    buf_b = {False: zfb_ref, True: zvb_ref}

    @pl.when(s % 2 == 0)
    def _():
        step(buf_a, buf_b)

    @pl.when(s % 2 == 1)
    def _():
        step(buf_b, buf_a)


def _mix_layer(x, gain, w_in, w_out, sinks, q_gain, k_gain, *, rb=256, proj_tile=256):
    s, d = x.shape
    nb = s // rb
    t = _position_tables(rb, nb)
    cur = lambda i: (jnp.minimum(i, nb - 1), 0)
    prv = lambda i: (jnp.maximum(i - 1, 0), 0)
    const2 = lambda i: (0, 0)
    resident = dict(pipeline_mode=pl.Buffered(1))
    base_tab = pl.BlockSpec((rb, HEAD_DIM), const2, **resident)
    smem = pl.BlockSpec(memory_space=pltpu.SMEM)
    return pl.pallas_call(
        functools.partial(_mix_layer_kernel, proj_tile=proj_tile),
        name="mix_layer",
        grid=(nb + 1,),
        in_specs=[
            smem, smem,
            pl.BlockSpec((rb, d), cur),
            pl.BlockSpec((1, d), const2),
            pl.BlockSpec(w_in.shape, const2, **resident),
            base_tab, base_tab, base_tab, base_tab,
            pl.BlockSpec((1,) + t["block_off"].shape[1:], lambda i: (jnp.maximum(i - 1, 0), 0, 0)),
            pl.BlockSpec(t["lane"].shape, const2),
            pl.BlockSpec((1, HEAD_DIM), const2), pl.BlockSpec((1, HEAD_DIM), const2),
            pl.BlockSpec(w_out.shape, const2, **resident),
        ],
        out_specs=pl.BlockSpec((rb, d), prv),
        out_shape=jax.ShapeDtypeStruct((s, d), F32),
        scratch_shapes=[
            pltpu.VMEM((rb, ZF_WIDTH), F32), pltpu.VMEM((rb, ZV_WIDTH), BF16),
            pltpu.VMEM((rb, ZF_WIDTH), F32), pltpu.VMEM((rb, ZV_WIDTH), BF16),
            pltpu.VMEM((rb, d), BF16),
            pltpu.VMEM((rb, d), F32),
            pltpu.VMEM((rb, MIX_WIDTH), BF16),
            pltpu.VMEM((RET_HEADS, HEAD_DIM, HEAD_DIM), F32),
            pltpu.VMEM((ATT_KV_HEADS, WINDOW, HEAD_DIM), BF16),
            pltpu.VMEM((ATT_KV_HEADS, WINDOW, HEAD_DIM), BF16),
        ],
        compiler_params=pltpu.CompilerParams(
            dimension_semantics=("arbitrary",),
            vmem_limit_bytes=VMEM_LIMIT_BYTES,
        ),
    )(sinks, t["gamma"], x, gain, w_in, t["cos_base_r"], t["sin_base_r"], t["cos_base_a"],
      t["sin_base_a"], t["block_off"], t["lane"], q_gain, k_gain, w_out)


def _position_tables(rb, nb):
    f32 = lambda a: jnp.asarray(np.asarray(a, np.float32))
    r = np.arange(rb, dtype=np.float64)[:, None]
    start = (np.arange(nb, dtype=np.float64) * rb)[:, None]

    inv_r = RET_THETA ** (-np.arange(0, HEAD_DIM, 2, dtype=np.float64) / HEAD_DIM)
    inv_r = np.concatenate([inv_r, inv_r])
    inv_a = ROPE_THETA ** (-np.arange(0, ROPE_DIM, 2, dtype=np.float64) / ROPE_DIM)
    inv_a = np.concatenate([inv_a, inv_a, np.zeros(HEAD_DIM - ROPE_DIM)])

    block_off = np.stack([np.cos(start * inv_r), np.sin(start * inv_r),
                          np.cos(start * inv_a), np.sin(start * inv_a)], axis=1)

    half = ROPE_DIM // 2
    lane = np.zeros((8, HEAD_DIM))
    lane[LANE_SIGN_R] = np.where(np.arange(HEAD_DIM) < HEAD_DIM // 2, -1.0, 1.0)
    lane[LANE_MASK_LO, :half] = -1.0
    lane[LANE_MASK_HI, half:ROPE_DIM] = 1.0

    log_g = np.log1p(-np.exp2(-5.0 - np.arange(RET_HEADS, dtype=np.float64)))
    gamma = np.stack([log_g, np.exp(CHUNK * log_g)])
    return {
        "cos_base_r": f32(np.cos(r * inv_r)), "sin_base_r": f32(np.sin(r * inv_r)),
        "cos_base_a": f32(np.cos(r * inv_a)), "sin_base_a": f32(np.sin(r * inv_a)),
        "block_off": f32(block_off), "lane": f32(lane), "gamma": f32(gamma),
    }


def kernel(x, ffn1_norm, ffn1_w_gate, ffn1_w_up, ffn1_w_down, mix_norm, w_in, q_norm, k_norm,
           attn_sinks, w_out, ffn2_norm, ffn2_w_gate, ffn2_w_up, ffn2_w_down):
    b, s, d = x.shape
    depth = ffn1_norm.shape[0]
    outs = []
    for bi in range(b):
        xb = x[bi]
        for l in range(depth):
            xb, w_in_bf16, w_out_bf16 = _ffn(
                xb, ffn1_norm[l][None], ffn1_w_gate[l], ffn1_w_up[l], ffn1_w_down[l],
                cast=(w_in[l], w_out[l]))
            xb = _mix_layer(xb, mix_norm[l][None], w_in_bf16, w_out_bf16, attn_sinks[l],
                            q_norm[l][None], k_norm[l][None])
            xb, = _ffn(xb, ffn2_norm[l][None], ffn2_w_gate[l], ffn2_w_up[l], ffn2_w_down[l])
        outs.append(xb)
    return jnp.stack(outs, axis=0)
```

```python
import functools
import itertools

import jax
import jax.numpy as jnp
import numpy as np
from jax import lax
from jax.experimental import pallas as pl
from jax.experimental.pallas import tpu as pltpu

D_MODEL = 2048
HEAD_DIM = 128
RET_HEADS = 8
ATT_Q_HEADS = 8
ATT_KV_HEADS = 2
GQA_GROUP = ATT_Q_HEADS // ATT_KV_HEADS
RET_WIDTH = RET_HEADS * HEAD_DIM
ATT_WIDTH = ATT_Q_HEADS * HEAD_DIM
KV_WIDTH = ATT_KV_HEADS * HEAD_DIM
MIX_WIDTH = RET_WIDTH + ATT_WIDTH
IN_WIDTH = 4 * RET_WIDTH + ATT_WIDTH + 2 * KV_WIDTH
CHUNK = 128
WINDOW = 128
ROPE_THETA = 500000.0
ROPE_DIM = HEAD_DIM // 4
RET_THETA = 10000.0
EPS = 1e-6

VMEM_LIMIT_BYTES = 62 * 1024 * 1024

F32 = jnp.float32
BF16 = jnp.bfloat16


def _rms_rows(x, gain):
    ms = jnp.mean(x * x, axis=-1, keepdims=True)
    return x * lax.rsqrt(ms + EPS) * gain


def _ffn_kernel(*refs, row_chunk, n_cast):
    x_hbm, gain_ref, wg_ref, wu_ref, wd_ref = refs[:5]
    cast_src = refs[5:5 + n_cast]
    o_ref = refs[5 + n_cast]
    cast_dst = refs[6 + n_cast:6 + 2 * n_cast]
    h_ref, xbuf_ref, sem = refs[6 + 2 * n_cast:]
    i = pl.program_id(0)
    j = pl.program_id(1)
    n_tiles = pl.num_programs(0)
    tm = o_ref.shape[0]

    def x_copy(tile):
        return pltpu.make_async_copy(x_hbm.at[pl.ds(tile * tm, tm), :], xbuf_ref, sem)

    @pl.when((i == 0) & (j == 0))
    def _():
        x_copy(0).start()

    @pl.when(j == 0)
    def _():
        x_copy(i).wait()
        for r in range(tm // row_chunk):
            rows = pl.ds(r * row_chunk, row_chunk)
            x = xbuf_ref[rows, :]
            h_ref[rows, :] = _rms_rows(x, gain_ref[...]).astype(BF16)
            o_ref[rows, :] = x

    @pl.when((j == 1) & (i + 1 < n_tiles))
    def _():
        x_copy(i + 1).start()

    for src, dst in zip(cast_src, cast_dst):
        dst[...] = src[...].astype(BF16)

    h = h_ref[...]
    g = jnp.dot(h, wg_ref[...].astype(BF16), preferred_element_type=F32)
    u = jnp.dot(h, wu_ref[...].astype(BF16), preferred_element_type=F32)
    a = (0.5 * (g * jax.nn.sigmoid(g)) * u).astype(BF16)
    o_ref[...] += jnp.dot(a, wd_ref[...].astype(BF16), preferred_element_type=F32)


def _ffn(x, gain, wg, wu, wd, cast=(), *, tm=1024, tf=512, row_chunk=128):
    s, d = x.shape
    d_ff = wg.shape[1]
    n_i, n_j = s // tm, d_ff // tf
    cast_specs = []
    for w in cast:
        n_blocks = max(n for n in range(1, n_i * n_j + 1)
                       if w.shape[0] % n == 0 and (w.shape[0] // n) % 16 == 0)
        index = lambda i, j, n_blocks=n_blocks: (jnp.minimum(i * n_j + j, n_blocks - 1), 0)
        cast_specs.append(pl.BlockSpec((w.shape[0] // n_blocks, w.shape[1]), index))
    outs = pl.pallas_call(
        functools.partial(_ffn_kernel, row_chunk=row_chunk, n_cast=len(cast)),
        name="ffn",
        grid=(n_i, n_j),
        in_specs=[
            pl.BlockSpec(memory_space=pl.ANY),
            pl.BlockSpec((1, d), lambda i, j: (0, 0)),
            pl.BlockSpec((d, tf), lambda i, j: (0, j)),
            pl.BlockSpec((d, tf), lambda i, j: (0, j)),
            pl.BlockSpec((tf, d), lambda i, j: (j, 0)),
            *cast_specs,
        ],
        out_specs=[pl.BlockSpec((tm, d), lambda i, j: (i, 0)), *cast_specs],
        out_shape=[jax.ShapeDtypeStruct((s, d), F32),
                   *[jax.ShapeDtypeStruct(w.shape, BF16) for w in cast]],
        scratch_shapes=[pltpu.VMEM((tm, d), BF16), pltpu.VMEM((tm, d), F32),
                        pltpu.SemaphoreType.DMA(())],
        compiler_params=pltpu.CompilerParams(
            dimension_semantics=("arbitrary", "arbitrary"),
            vmem_limit_bytes=VMEM_LIMIT_BYTES,
        ),
    )(x, gain, wg, wu, wd, *cast)
    return tuple(outs)


_SEGMENTS = (("q_r", RET_WIDTH, False), ("k_r", RET_WIDTH, False), ("v_r", RET_WIDTH, True),
             ("g_r", RET_WIDTH, False), ("q_a", ATT_WIDTH, False), ("k_a", KV_WIDTH, False),
             ("v_a", KV_WIDTH, True))


def _z_layout():
    layout, w_off, widths = {}, 0, {False: 0, True: 0}
    for name, width, is_value in _SEGMENTS:
        layout[name] = (is_value, widths[is_value], w_off)
        widths[is_value] += width
        w_off += width
    return layout, widths[False], widths[True]


_Z_LAYOUT, ZF_WIDTH, ZV_WIDTH = _z_layout()


def _z_dest(w_col):
    for name, width, _ in _SEGMENTS:
        is_value, z_off, w_off = _Z_LAYOUT[name]
        if w_off <= w_col < w_off + width:
            return is_value, z_off + (w_col - w_off)
    raise ValueError(w_col)


def _dot_nt(a, b):
    return lax.dot_general(a, b, (((1,), (1,)), ((), ())), preferred_element_type=F32)


def _dot_tn(a, b):
    return lax.dot_general(a, b, (((0,), (0,)), ((), ())), preferred_element_type=F32)


LANE_SIGN_R, LANE_MASK_LO, LANE_MASK_HI = 0, 1, 2
OFF_COS_R, OFF_SIN_R, OFF_COS_A, OFF_SIN_A = 0, 1, 2, 3
GAM_LOG, GAM_CHUNK = 0, 1


def _mix_chunk(zhead, rows, not_first, sink_ref, gam_ref, base_refs, off, lane, qn_ref, kn_ref,
               y_ref, state_ref, kprev_ref, vprev_ref):
    cb_r_ref, sb_r_ref, cb_a_ref, sb_a_ref = base_refs
    row = lambda t, k: t[k:k + 1, :]

    cb, sb = cb_r_ref[rows, :], sb_r_ref[rows, :]
    co, so = row(off, OFF_COS_R), row(off, OFF_SIN_R)
    cos_r = cb * co - sb * so
    sin_r = (sb * co + cb * so) * row(lane, LANE_SIGN_R)

    def rot_r(t):
        return t * cos_r + pltpu.roll(t, HEAD_DIM // 2, 1) * sin_r

    n = lax.broadcasted_iota(jnp.int32, (CHUNK, CHUNK), 0)
    m = lax.broadcasted_iota(jnp.int32, (CHUNK, CHUNK), 1)
    causal = n >= m
    lag = jnp.maximum(n - m, 0).astype(F32)
    n_plus_1 = (n + 1).astype(F32)
    to_end = (CHUNK - 1 - n).astype(F32)

    for h in range(RET_HEADS):
        log_g = gam_ref[GAM_LOG, h]
        q = rot_r(zhead("q_r", h))
        k = rot_r(zhead("k_r", h)) * (HEAD_DIM ** -0.5)
        vb = zhead("v_r", h)
        decay = jnp.where(causal, jnp.exp(lag * log_g), 0.0)
        scores = _dot_nt(q.astype(BF16), k.astype(BF16)) * decay
        prev = state_ref[h]
        q_cross = (q * jnp.exp(n_plus_1 * log_g)).astype(BF16)
        lhs = jnp.concatenate([scores.astype(BF16), q_cross], axis=1)
        rhs = jnp.concatenate([vb, prev.astype(BF16)], axis=0)
        y = jnp.dot(lhs, rhs, preferred_element_type=F32)
        chunk_kv = _dot_tn((k * jnp.exp(to_end * log_g)).astype(BF16), vb)
        state_ref[h] = prev * gam_ref[GAM_CHUNK, h] + chunk_kv
        y = y * lax.rsqrt(jnp.mean(y * y, axis=-1, keepdims=True) + EPS)
        g = zhead("g_r", h)
        y_ref[rows, h * HEAD_DIM:(h + 1) * HEAD_DIM] = ((g * jax.nn.sigmoid(g)) * y).astype(BF16)
        yield

    cb, sb = cb_a_ref[rows, :], sb_a_ref[rows, :]
    co, so = row(off, OFF_COS_A), row(off, OFF_SIN_A)
    cos_a = cb * co - sb * so
    sin_a = sb * co + cb * so
    sin_lo = sin_a * row(lane, LANE_MASK_LO)
    sin_hi = sin_a * row(lane, LANE_MASK_HI)
    half = ROPE_DIM // 2

    def rot_a(t):
        return (t * cos_a + pltpu.roll(t, HEAD_DIM - half, 1) * sin_lo
                + pltpu.roll(t, half, 1) * sin_hi)

    qi = lax.broadcasted_iota(jnp.int32, (WINDOW, 2 * WINDOW), 0)
    kj = lax.broadcasted_iota(jnp.int32, (WINDOW, 2 * WINDOW), 1)
    rel = WINDOW + qi - kj
    mask = (rel >= 0) & (rel < WINDOW) & ((kj >= WINDOW) | not_first)
    neg = jnp.finfo(F32).min

    for kh in range(ATT_KV_HEADS):
        kb = rot_a(_rms_rows(zhead("k_a", kh), kn_ref[...])).astype(BF16)
        vb = zhead("v_a", kh)
        kk = jnp.concatenate([kprev_ref[kh], kb], axis=0)
        vv = jnp.concatenate([vprev_ref[kh], vb], axis=0)
        heads = range(kh * GQA_GROUP, (kh + 1) * GQA_GROUP)
        qs = jnp.concatenate(
            [rot_a(_rms_rows(zhead("q_a", qh), qn_ref[...])).astype(BF16) for qh in heads], axis=0)
        s_all = _dot_nt(qs, kk) * (HEAD_DIM ** -0.5)
        probs, inv_denoms = [], []
        for gq, qh in enumerate(heads):
            s = jnp.where(mask, s_all[gq * WINDOW:(gq + 1) * WINDOW, :], neg)
            sink = sink_ref[qh]
            mx = jnp.maximum(jnp.max(s, axis=-1, keepdims=True), sink)
            p = jnp.exp(s - mx)
            denom = jnp.sum(p, axis=-1, keepdims=True) + jnp.exp(sink - mx)
            probs.append(p.astype(BF16))
            inv_denoms.append(1.0 / denom)
        o_all = jnp.dot(jnp.concatenate(probs, axis=0), vv, preferred_element_type=F32)
        for gq, qh in enumerate(heads):
            o = o_all[gq * WINDOW:(gq + 1) * WINDOW, :] * inv_denoms[gq]
            col = RET_WIDTH + qh * HEAD_DIM
            y_ref[rows, col:col + HEAD_DIM] = o.astype(BF16)
        kprev_ref[kh] = kb
        vprev_ref[kh] = vb
        yield


def _mix_layer_kernel(sink_ref, gam_ref, xn_ref, gain_ref, win_ref, cb_r_ref, sb_r_ref,
                      cb_a_ref, sb_a_ref, off_ref, lane_ref, qn_ref, kn_ref, wout_ref, o_ref,
                      zfa_ref, zva_ref, zfb_ref, zvb_ref, h_ref, xprev_ref, y_ref, state_ref,
                      kprev_ref, vprev_ref, *, proj_tile):
    s = pl.program_id(0)
    rb = xn_ref.shape[0]

    @pl.when(s == 0)
    def _():
        zfb_ref[...] = jnp.zeros_like(zfb_ref)
        zvb_ref[...] = jnp.zeros_like(zvb_ref)
        xprev_ref[...] = jnp.zeros_like(xprev_ref)

    @pl.when(s <= 1)
    def _():
        state_ref[...] = jnp.zeros_like(state_ref)
        kprev_ref[...] = jnp.zeros_like(kprev_ref)
        vprev_ref[...] = jnp.zeros_like(vprev_ref)

    base_refs = (cb_r_ref, sb_r_ref, cb_a_ref, sb_a_ref)

    def step(zw, zr):
        off = off_ref[0]
        lane = lane_ref[...]

        def chunk_pieces(cc):
            rows = slice(cc * CHUNK, (cc + 1) * CHUNK)

            def zhead(name, hd):
                is_value, z_off, _ = _Z_LAYOUT[name]
                return zr[is_value][rows, z_off + hd * HEAD_DIM: z_off + (hd + 1) * HEAD_DIM]

            not_first = (s > 1) if cc == 0 else True
            return _mix_chunk(zhead, rows, not_first, sink_ref, gam_ref, base_refs, off, lane,
                              qn_ref, kn_ref, y_ref, state_ref, kprev_ref, vprev_ref)

        pieces = itertools.chain(*[chunk_pieces(cc) for cc in range(rb // CHUNK)])

        h_ref[...] = _rms_rows(xn_ref[...], gain_ref[...]).astype(BF16)
        for t in range(IN_WIDTH // proj_tile):
            w_col = t * proj_tile
            is_value, z_col = _z_dest(w_col)
            dst = zw[is_value]
            tile = jnp.dot(h_ref[...], win_ref[:, w_col:w_col + proj_tile],
                           preferred_element_type=F32)
            dst[:, z_col:z_col + proj_tile] = tile.astype(dst.dtype)
            next(pieces, None)
        for _ in pieces:
            pass

        o_ref[...] = xprev_ref[...] + jnp.dot(y_ref[...], wout_ref[...],
                                              preferred_element_type=F32)
        xprev_ref[...] = xn_ref[...]

    buf_a = {False: zfa_ref, True: zva_ref}
    buf_b = {False: zfb_ref, True: zvb_ref}

    @pl.when(s % 2 == 0)
    def _():
        step(buf_a, buf_b)

    @pl.when(s % 2 == 1)
    def _():
        step(buf_b, buf_a)


def _mix_layer(x, gain, w_in, w_out, sinks, q_gain, k_gain, *, rb=256, proj_tile=256):
    s, d = x.shape
    nb = s // rb
    t = _position_tables(rb, nb)
    cur = lambda i: (jnp.minimum(i, nb - 1), 0)
    prv = lambda i: (jnp.maximum(i - 1, 0), 0)
    const2 = lambda i: (0, 0)
    resident = dict(pipeline_mode=pl.Buffered(1))
    base_tab = pl.BlockSpec((rb, HEAD_DIM), const2, **resident)
    smem = pl.BlockSpec(memory_space=pltpu.SMEM)
    return pl.pallas_call(
        functools.partial(_mix_layer_kernel, proj_tile=proj_tile),
        name="mix_layer",
        grid=(nb + 1,),
        in_specs=[
            smem, smem,
            pl.BlockSpec((rb, d), cur),
            pl.BlockSpec((1, d), const2),
            pl.BlockSpec(w_in.shape, const2, **resident),
            base_tab, base_tab, base_tab, base_tab,
            pl.BlockSpec((1,) + t["block_off"].shape[1:], lambda i: (jnp.maximum(i - 1, 0), 0, 0)),
            pl.BlockSpec(t["lane"].shape, const2),
            pl.BlockSpec((1, HEAD_DIM), const2), pl.BlockSpec((1, HEAD_DIM), const2),
            pl.BlockSpec(w_out.shape, const2, **resident),
        ],
        out_specs=pl.BlockSpec((rb, d), prv),
        out_shape=jax.ShapeDtypeStruct((s, d), F32),
        scratch_shapes=[
            pltpu.VMEM((rb, ZF_WIDTH), F32), pltpu.VMEM((rb, ZV_WIDTH), BF16),
            pltpu.VMEM((rb, ZF_WIDTH), F32), pltpu.VMEM((rb, ZV_WIDTH), BF16),
            pltpu.VMEM((rb, d), BF16),
            pltpu.VMEM((rb, d), F32),
            pltpu.VMEM((rb, MIX_WIDTH), BF16),
            pltpu.VMEM((RET_HEADS, HEAD_DIM, HEAD_DIM), F32),
            pltpu.VMEM((ATT_KV_HEADS, WINDOW, HEAD_DIM), BF16),
            pltpu.VMEM((ATT_KV_HEADS, WINDOW, HEAD_DIM), BF16),
        ],
        compiler_params=pltpu.CompilerParams(
            dimension_semantics=("arbitrary",),
            vmem_limit_bytes=VMEM_LIMIT_BYTES,
        ),
    )(sinks, t["gamma"], x, gain, w_in, t["cos_base_r"], t["sin_base_r"], t["cos_base_a"],
      t["sin_base_a"], t["block_off"], t["lane"], q_gain, k_gain, w_out)


def _position_tables(rb, nb):
    f32 = lambda a: jnp.asarray(np.asarray(a, np.float32))
    r = np.arange(rb, dtype=np.float64)[:, None]
    start = (np.arange(nb, dtype=np.float64) * rb)[:, None]

    inv_r = RET_THETA ** (-np.arange(0, HEAD_DIM, 2, dtype=np.float64) / HEAD_DIM)
    inv_r = np.concatenate([inv_r, inv_r])
    inv_a = ROPE_THETA ** (-np.arange(0, ROPE_DIM, 2, dtype=np.float64) / ROPE_DIM)
    inv_a = np.concatenate([inv_a, inv_a, np.zeros(HEAD_DIM - ROPE_DIM)])

    block_off = np.stack([np.cos(start * inv_r), np.sin(start * inv_r),
                          np.cos(start * inv_a), np.sin(start * inv_a)], axis=1)

    half = ROPE_DIM // 2
    lane = np.zeros((8, HEAD_DIM))
    lane[LANE_SIGN_R] = np.where(np.arange(HEAD_DIM) < HEAD_DIM // 2, -1.0, 1.0)
    lane[LANE_MASK_LO, :half] = -1.0
    lane[LANE_MASK_HI, half:ROPE_DIM] = 1.0

    log_g = np.log1p(-np.exp2(-5.0 - np.arange(RET_HEADS, dtype=np.float64)))
    gamma = np.stack([log_g, np.exp(CHUNK * log_g)])
    return {
        "cos_base_r": f32(np.cos(r * inv_r)), "sin_base_r": f32(np.sin(r * inv_r)),
        "cos_base_a": f32(np.cos(r * inv_a)), "sin_base_a": f32(np.sin(r * inv_a)),
        "block_off": f32(block_off), "lane": f32(lane), "gamma": f32(gamma),
    }


def kernel(x, ffn1_norm, ffn1_w_gate, ffn1_w_up, ffn1_w_down, mix_norm, w_in, q_norm, k_norm,
           attn_sinks, w_out, ffn2_norm, ffn2_w_gate, ffn2_w_up, ffn2_w_down):
    b, s, d = x.shape
    depth = ffn1_norm.shape[0]
    outs = []
    for bi in range(b):
        xb = x[bi]
        for l in range(depth):
            xb, w_in_bf16, w_out_bf16 = _ffn(
                xb, ffn1_norm[l][None], ffn1_w_gate[l], ffn1_w_up[l], ffn1_w_down[l],
                cast=(w_in[l], w_out[l]))
            xb = _mix_layer(xb, mix_norm[l][None], w_in_bf16, w_out_bf16, attn_sinks[l],
                            q_norm[l][None], k_norm[l][None])
            xb, = _ffn(xb, ffn2_norm[l][None], ffn2_w_gate[l], ffn2_w_up[l], ffn2_w_down[l])
        outs.append(xb)
    return jnp.stack(outs, axis=0)
```

```python
import functools
import itertools

import jax
import jax.numpy as jnp
import numpy as np
from jax import lax
from jax.experimental import pallas as pl
from jax.experimental.pallas import tpu as pltpu

D_MODEL = 2048
HEAD_DIM = 128
RET_HEADS = 8
ATT_Q_HEADS = 8
ATT_KV_HEADS = 2
GQA_GROUP = ATT_Q_HEADS // ATT_KV_HEADS
RET_WIDTH = RET_HEADS * HEAD_DIM
ATT_WIDTH = ATT_Q_HEADS * HEAD_DIM
KV_WIDTH = ATT_KV_HEADS * HEAD_DIM
MIX_WIDTH = RET_WIDTH + ATT_WIDTH
IN_WIDTH = 4 * RET_WIDTH + ATT_WIDTH + 2 * KV_WIDTH
CHUNK = 128
WINDOW = 128
ROPE_THETA = 500000.0
ROPE_DIM = HEAD_DIM // 4
RET_THETA = 10000.0
EPS = 1e-6

VMEM_LIMIT_BYTES = 62 * 1024 * 1024

F32 = jnp.float32
BF16 = jnp.bfloat16


def _rms_rows(x, gain):
    ms = jnp.mean(x * x, axis=-1, keepdims=True)
    return x * lax.rsqrt(ms + EPS) * gain


def _ffn_kernel(*refs, row_chunk, n_cast):
    x_hbm, gain_ref, wg_ref, wu_ref, wd_ref = refs[:5]
    cast_src = refs[5:5 + n_cast]
    o_ref = refs[5 + n_cast]
    cast_dst = refs[6 + n_cast:6 + 2 * n_cast]
    h_ref, xbuf_ref, sem = refs[6 + 2 * n_cast:]
    i = pl.program_id(0)
    j = pl.program_id(1)
    n_tiles = pl.num_programs(0)
    tm = o_ref.shape[0]

    def x_copy(tile):
        return pltpu.make_async_copy(x_hbm.at[pl.ds(tile * tm, tm), :], xbuf_ref, sem)

    @pl.when((i == 0) & (j == 0))
    def _():
        x_copy(0).start()

    @pl.when(j == 0)
    def _():
        x_copy(i).wait()
        for r in range(tm // row_chunk):
            rows = pl.ds(r * row_chunk, row_chunk)
            x = xbuf_ref[rows, :]
            h_ref[rows, :] = _rms_rows(x, gain_ref[...]).astype(BF16)
            o_ref[rows, :] = x

    @pl.when((j == 1) & (i + 1 < n_tiles))
    def _():
        x_copy(i + 1).start()

    for src, dst in zip(cast_src, cast_dst):
        dst[...] = src[...].astype(BF16)

    h = h_ref[...]
    g = jnp.dot(h, wg_ref[...].astype(BF16), preferred_element_type=F32)
    u = jnp.dot(h, wu_ref[...].astype(BF16), preferred_element_type=F32)
    a = (0.5 * (g * jax.nn.sigmoid(g)) * u).astype(BF16)
    o_ref[...] += jnp.dot(a, wd_ref[...].astype(BF16), preferred_element_type=F32)


def _ffn(x, gain, wg, wu, wd, cast=(), *, tm=1024, tf=512, row_chunk=128):
    s, d = x.shape
    d_ff = wg.shape[1]
    n_i, n_j = s // tm, d_ff // tf
    cast_specs = []
    for w in cast:
        n_blocks = max(n for n in range(1, n_i * n_j + 1)
                       if w.shape[0] % n == 0 and (w.shape[0] // n) % 16 == 0)
        index = lambda i, j, n_blocks=n_blocks: (jnp.minimum(i * n_j + j, n_blocks - 1), 0)
        cast_specs.append(pl.BlockSpec((w.shape[0] // n_blocks, w.shape[1]), index))
    outs = pl.pallas_call(
        functools.partial(_ffn_kernel, row_chunk=row_chunk, n_cast=len(cast)),
        name="ffn",
        grid=(n_i, n_j),
        in_specs=[
            pl.BlockSpec(memory_space=pl.ANY),
            pl.BlockSpec((1, d), lambda i, j: (0, 0)),
            pl.BlockSpec((d, tf), lambda i, j: (0, j)),
            pl.BlockSpec((d, tf), lambda i, j: (0, j)),
            pl.BlockSpec((tf, d), lambda i, j: (j, 0)),
            *cast_specs,
        ],
        out_specs=[pl.BlockSpec((tm, d), lambda i, j: (i, 0)), *cast_specs],
        out_shape=[jax.ShapeDtypeStruct((s, d), F32),
                   *[jax.ShapeDtypeStruct(w.shape, BF16) for w in cast]],
        scratch_shapes=[pltpu.VMEM((tm, d), BF16), pltpu.VMEM((tm, d), F32),
                        pltpu.SemaphoreType.DMA(())],
        compiler_params=pltpu.CompilerParams(
            dimension_semantics=("arbitrary", "arbitrary"),
            vmem_limit_bytes=VMEM_LIMIT_BYTES,
        ),
    )(x, gain, wg, wu, wd, *cast)
    return tuple(outs)


_SEGMENTS = (("q_r", RET_WIDTH, False), ("k_r", RET_WIDTH, False), ("v_r", RET_WIDTH, True),
             ("g_r", RET_WIDTH, False), ("q_a", ATT_WIDTH, False), ("k_a", KV_WIDTH, False),
             ("v_a", KV_WIDTH, True))


def _z_layout():
    layout, w_off, widths = {}, 0, {False: 0, True: 0}
    for name, width, is_value in _SEGMENTS:
        layout[name] = (is_value, widths[is_value], w_off)
        widths[is_value] += width
        w_off += width
    return layout, widths[False], widths[True]


_Z_LAYOUT, ZF_WIDTH, ZV_WIDTH = _z_layout()


def _z_dest(w_col):
    for name, width, _ in _SEGMENTS:
        is_value, z_off, w_off = _Z_LAYOUT[name]
        if w_off <= w_col < w_off + width:
            return is_value, z_off + (w_col - w_off)
    raise ValueError(w_col)


def _dot_nt(a, b):
    return lax.dot_general(a, b, (((1,), (1,)), ((), ())), preferred_element_type=F32)


def _dot_tn(a, b):
    return lax.dot_general(a, b, (((0,), (0,)), ((), ())), preferred_element_type=F32)


LANE_SIGN_R, LANE_MASK_LO, LANE_MASK_HI = 0, 1, 2
OFF_COS_R, OFF_SIN_R, OFF_COS_A, OFF_SIN_A = 0, 1, 2, 3
GAM_LOG, GAM_CHUNK = 0, 1


def _mix_chunk(zhead, rows, not_first, sink_ref, gam_ref, base_refs, off, lane, qn_ref, kn_ref,
               y_ref, state_ref, kprev_ref, vprev_ref):
    cb_r_ref, sb_r_ref, cb_a_ref, sb_a_ref = base_refs
    row = lambda t, k: t[k:k + 1, :]

    cb, sb = cb_r_ref[rows, :], sb_r_ref[rows, :]
    co, so = row(off, OFF_COS_R), row(off, OFF_SIN_R)
    cos_r = cb * co - sb * so
    sin_r = (sb * co + cb * so) * row(lane, LANE_SIGN_R)

    def rot_r(t):
        return t * cos_r + pltpu.roll(t, HEAD_DIM // 2, 1) * sin_r

    n = lax.broadcasted_iota(jnp.int32, (CHUNK, CHUNK), 0)
    m = lax.broadcasted_iota(jnp.int32, (CHUNK, CHUNK), 1)
    causal = n >= m
    lag = jnp.maximum(n - m, 0).astype(F32)
    n_plus_1 = (n + 1).astype(F32)
    to_end = (CHUNK - 1 - n).astype(F32)

    for h in range(RET_HEADS):
        log_g = gam_ref[GAM_LOG, h]
        q = rot_r(zhead("q_r", h))
        k = rot_r(zhead("k_r", h)) * (HEAD_DIM ** -0.5)
        vb = zhead("v_r", h)
        decay = jnp.where(causal, jnp.exp(lag * log_g), 0.0)
        scores = _dot_nt(q.astype(BF16), k.astype(BF16)) * decay
        prev = state_ref[h]
        q_cross = (q * jnp.exp(n_plus_1 * log_g)).astype(BF16)
        lhs = jnp.concatenate([scores.astype(BF16), q_cross], axis=1)
        rhs = jnp.concatenate([vb, prev.astype(BF16)], axis=0)
        y = jnp.dot(lhs, rhs, preferred_element_type=F32)
        chunk_kv = _dot_tn((k * jnp.exp(to_end * log_g)).astype(BF16), vb)
        state_ref[h] = prev * gam_ref[GAM_CHUNK, h] + chunk_kv
        y = y * lax.rsqrt(jnp.mean(y * y, axis=-1, keepdims=True) + EPS)
        g = zhead("g_r", h)
        y_ref[rows, h * HEAD_DIM:(h + 1) * HEAD_DIM] = ((g * jax.nn.sigmoid(g)) * y).astype(BF16)
        yield

    cb, sb = cb_a_ref[rows, :], sb_a_ref[rows, :]
    co, so = row(off, OFF_COS_A), row(off, OFF_SIN_A)
    cos_a = cb * co - sb * so
    sin_a = sb * co + cb * so
    sin_lo = sin_a * row(lane, LANE_MASK_LO)
    sin_hi = sin_a * row(lane, LANE_MASK_HI)
    half = ROPE_DIM // 2

    def rot_a(t):
        return (t * cos_a + pltpu.roll(t, HEAD_DIM - half, 1) * sin_lo
                + pltpu.roll(t, half, 1) * sin_hi)

    qi = lax.broadcasted_iota(jnp.int32, (WINDOW, 2 * WINDOW), 0)
    kj = lax.broadcasted_iota(jnp.int32, (WINDOW, 2 * WINDOW), 1)
    rel = WINDOW + qi - kj
    mask = (rel >= 0) & (rel < WINDOW) & ((kj >= WINDOW) | not_first)
    neg = jnp.finfo(F32).min

    for kh in range(ATT_KV_HEADS):
        kb = rot_a(_rms_rows(zhead("k_a", kh), kn_ref[...])).astype(BF16)
        vb = zhead("v_a", kh)
        kk = jnp.concatenate([kprev_ref[kh], kb], axis=0)
        vv = jnp.concatenate([vprev_ref[kh], vb], axis=0)
        heads = range(kh * GQA_GROUP, (kh + 1) * GQA_GROUP)
        qs = jnp.concatenate(
            [rot_a(_rms_rows(zhead("q_a", qh), qn_ref[...])).astype(BF16) for qh in heads], axis=0)
        s_all = _dot_nt(qs, kk) * (HEAD_DIM ** -0.5)
        probs, inv_denoms = [], []
        for gq, qh in enumerate(heads):
            s = jnp.where(mask, s_all[gq * WINDOW:(gq + 1) * WINDOW, :], neg)
            sink = sink_ref[qh]
            mx = jnp.maximum(jnp.max(s, axis=-1, keepdims=True), sink)
            p = jnp.exp(s - mx)
            denom = jnp.sum(p, axis=-1, keepdims=True) + jnp.exp(sink - mx)
            probs.append(p.astype(BF16))
            inv_denoms.append(1.0 / denom)
        o_all = jnp.dot(jnp.concatenate(probs, axis=0), vv, preferred_element_type=F32)
        for gq, qh in enumerate(heads):
            o = o_all[gq * WINDOW:(gq + 1) * WINDOW, :] * inv_denoms[gq]
            col = RET_WIDTH + qh * HEAD_DIM
            y_ref[rows, col:col + HEAD_DIM] = o.astype(BF16)
        kprev_ref[kh] = kb
        vprev_ref[kh] = vb
        yield


def _mix_layer_kernel(sink_ref, gam_ref, xn_ref, gain_ref, win_ref, cb_r_ref, sb_r_ref,
                      cb_a_ref, sb_a_ref, off_ref, lane_ref, qn_ref, kn_ref, wout_ref, o_ref,
                      zfa_ref, zva_ref, zfb_ref, zvb_ref, h_ref, xprev_ref, y_ref, state_ref,
                      kprev_ref, vprev_ref, *, proj_tile):
    s = pl.program_id(0)
    rb = xn_ref.shape[0]

    @pl.when(s == 0)
    def _():
        zfb_ref[...] = jnp.zeros_like(zfb_ref)
        zvb_ref[...] = jnp.zeros_like(zvb_ref)
        xprev_ref[...] = jnp.zeros_like(xprev_ref)

    @pl.when(s <= 1)
    def _():
        state_ref[...] = jnp.zeros_like(state_ref)
        kprev_ref[...] = jnp.zeros_like(kprev_ref)
        vprev_ref[...] = jnp.zeros_like(vprev_ref)

    base_refs = (cb_r_ref, sb_r_ref, cb_a_ref, sb_a_ref)

    def step(zw, zr):
        off = off_ref[0]
        lane = lane_ref[...]

        def chunk_pieces(cc):
            rows = slice(cc * CHUNK, (cc + 1) * CHUNK)

            def zhead(name, hd):
                is_value, z_off, _ = _Z_LAYOUT[name]
                return zr[is_value][rows, z_off + hd * HEAD_DIM: z_off + (hd + 1) * HEAD_DIM]

            not_first = (s > 1) if cc == 0 else True
            return _mix_chunk(zhead, rows, not_first, sink_ref, gam_ref, base_refs, off, lane,
                              qn_ref, kn_ref, y_ref, state_ref, kprev_ref, vprev_ref)

        pieces = itertools.chain(*[chunk_pieces(cc) for cc in range(rb // CHUNK)])

        h_ref[...] = _rms_rows(xn_ref[...], gain_ref[...]).astype(BF16)
        for t in range(IN_WIDTH // proj_tile):
            w_col = t * proj_tile
            is_value, z_col = _z_dest(w_col)
            dst = zw[is_value]
            tile = jnp.dot(h_ref[...], win_ref[:, w_col:w_col + proj_tile],
                           preferred_element_type=F32)
            dst[:, z_col:z_col + proj_tile] = tile.astype(dst.dtype)
            next(pieces, None)
        for _ in pieces:
            pass

        o_ref[...] = xprev_ref[...] + jnp.dot(y_ref[...], wout_ref[...],
                                              preferred_element_type=F32)
        xprev_ref[...] = xn_ref[...]
        for is_value in (False, True):
            zr[is_value][...] = zw[is_value][...]

    step({False: zfa_ref, True: zva_ref}, {False: zfb_ref, True: zvb_ref})


def _mix_layer(x, gain, w_in, w_out, sinks, q_gain, k_gain, *, rb=256, proj_tile=256):
    s, d = x.shape
    nb = s // rb
    t = _position_tables(rb, nb)
    cur = lambda i: (jnp.minimum(i, nb - 1), 0)
    prv = lambda i: (jnp.maximum(i - 1, 0), 0)
    const2 = lambda i: (0, 0)
    resident = dict(pipeline_mode=pl.Buffered(1))
    base_tab = pl.BlockSpec((rb, HEAD_DIM), const2, **resident)
    smem = pl.BlockSpec(memory_space=pltpu.SMEM)
    return pl.pallas_call(
        functools.partial(_mix_layer_kernel, proj_tile=proj_tile),
        name="mix_layer",
        grid=(nb + 1,),
        in_specs=[
            smem, smem,
            pl.BlockSpec((rb, d), cur),
            pl.BlockSpec((1, d), const2),
            pl.BlockSpec(w_in.shape, const2, **resident),
            base_tab, base_tab, base_tab, base_tab,
            pl.BlockSpec((1,) + t["block_off"].shape[1:], lambda i: (jnp.maximum(i - 1, 0), 0, 0)),
            pl.BlockSpec(t["lane"].shape, const2),
            pl.BlockSpec((1, HEAD_DIM), const2), pl.BlockSpec((1, HEAD_DIM), const2),
            pl.BlockSpec(w_out.shape, const2, **resident),
        ],
        out_specs=pl.BlockSpec((rb, d), prv),
        out_shape=jax.ShapeDtypeStruct((s, d), F32),
        scratch_shapes=[
            pltpu.VMEM((rb, ZF_WIDTH), F32), pltpu.VMEM((rb, ZV_WIDTH), BF16),
            pltpu.VMEM((rb, ZF_WIDTH), F32), pltpu.VMEM((rb, ZV_WIDTH), BF16),
            pltpu.VMEM((rb, d), BF16),
            pltpu.VMEM((rb, d), F32),
            pltpu.VMEM((rb, MIX_WIDTH), BF16),
            pltpu.VMEM((RET_HEADS, HEAD_DIM, HEAD_DIM), F32),
            pltpu.VMEM((ATT_KV_HEADS, WINDOW, HEAD_DIM), BF16),
            pltpu.VMEM((ATT_KV_HEADS, WINDOW, HEAD_DIM), BF16),
        ],
        compiler_params=pltpu.CompilerParams(
            dimension_semantics=("arbitrary",),
            vmem_limit_bytes=VMEM_LIMIT_BYTES,
        ),
    )(sinks, t["gamma"], x, gain, w_in, t["cos_base_r"], t["sin_base_r"], t["cos_base_a"],
      t["sin_base_a"], t["block_off"], t["lane"], q_gain, k_gain, w_out)


def _position_tables(rb, nb):
    f32 = lambda a: jnp.asarray(np.asarray(a, np.float32))
    r = np.arange(rb, dtype=np.float64)[:, None]
    start = (np.arange(nb, dtype=np.float64) * rb)[:, None]

    inv_r = RET_THETA ** (-np.arange(0, HEAD_DIM, 2, dtype=np.float64) / HEAD_DIM)
    inv_r = np.concatenate([inv_r, inv_r])
    inv_a = ROPE_THETA ** (-np.arange(0, ROPE_DIM, 2, dtype=np.float64) / ROPE_DIM)
    inv_a = np.concatenate([inv_a, inv_a, np.zeros(HEAD_DIM - ROPE_DIM)])

    block_off = np.stack([np.cos(start * inv_r), np.sin(start * inv_r),
                          np.cos(start * inv_a), np.sin(start * inv_a)], axis=1)

    half = ROPE_DIM // 2
    lane = np.zeros((8, HEAD_DIM))
    lane[LANE_SIGN_R] = np.where(np.arange(HEAD_DIM) < HEAD_DIM // 2, -1.0, 1.0)
    lane[LANE_MASK_LO, :half] = -1.0
    lane[LANE_MASK_HI, half:ROPE_DIM] = 1.0

    log_g = np.log1p(-np.exp2(-5.0 - np.arange(RET_HEADS, dtype=np.float64)))
    gamma = np.stack([log_g, np.exp(CHUNK * log_g)])
    return {
        "cos_base_r": f32(np.cos(r * inv_r)), "sin_base_r": f32(np.sin(r * inv_r)),
        "cos_base_a": f32(np.cos(r * inv_a)), "sin_base_a": f32(np.sin(r * inv_a)),
        "block_off": f32(block_off), "lane": f32(lane), "gamma": f32(gamma),
    }


def kernel(x, ffn1_norm, ffn1_w_gate, ffn1_w_up, ffn1_w_down, mix_norm, w_in, q_norm, k_norm,
           attn_sinks, w_out, ffn2_norm, ffn2_w_gate, ffn2_w_up, ffn2_w_down):
    b, s, d = x.shape
    depth = ffn1_norm.shape[0]
    outs = []
    for bi in range(b):
        xb = x[bi]
        for l in range(depth):
            xb, w_in_bf16, w_out_bf16 = _ffn(
                xb, ffn1_norm[l][None], ffn1_w_gate[l], ffn1_w_up[l], ffn1_w_down[l],
                cast=(w_in[l], w_out[l]))
            xb = _mix_layer(xb, mix_norm[l][None], w_in_bf16, w_out_bf16, attn_sinks[l],
                            q_norm[l][None], k_norm[l][None])
            xb, = _ffn(xb, ffn2_norm[l][None], ffn2_w_gate[l], ffn2_w_up[l], ffn2_w_down[l])
        outs.append(xb)
    return jnp.stack(outs, axis=0)
```

```python
import functools
import itertools

import jax
import jax.numpy as jnp
import numpy as np
from jax import lax
from jax.experimental import pallas as pl
from jax.experimental.pallas import tpu as pltpu

D_MODEL = 2048
HEAD_DIM = 128
RET_HEADS = 8
ATT_Q_HEADS = 8
ATT_KV_HEADS = 2
GQA_GROUP = ATT_Q_HEADS // ATT_KV_HEADS
RET_WIDTH = RET_HEADS * HEAD_DIM
ATT_WIDTH = ATT_Q_HEADS * HEAD_DIM
KV_WIDTH = ATT_KV_HEADS * HEAD_DIM
MIX_WIDTH = RET_WIDTH + ATT_WIDTH
IN_WIDTH = 4 * RET_WIDTH + ATT_WIDTH + 2 * KV_WIDTH
CHUNK = 128
WINDOW = 128
ROPE_THETA = 500000.0
ROPE_DIM = HEAD_DIM // 4
RET_THETA = 10000.0
EPS = 1e-6

VMEM_LIMIT_BYTES = 62 * 1024 * 1024

F32 = jnp.float32
BF16 = jnp.bfloat16


def _rms_rows(x, gain):
    ms = jnp.mean(x * x, axis=-1, keepdims=True)
    return x * lax.rsqrt(ms + EPS) * gain


def _ffn_kernel(*refs, row_chunk, n_cast):
    x_hbm, gain_ref, wg_ref, wu_ref, wd_ref = refs[:5]
    cast_src = refs[5:5 + n_cast]
    o_ref = refs[5 + n_cast]
    cast_dst = refs[6 + n_cast:6 + 2 * n_cast]
    h_ref, xbuf_ref, sem = refs[6 + 2 * n_cast:]
    i = pl.program_id(0)
    j = pl.program_id(1)
    n_tiles = pl.num_programs(0)
    tm = o_ref.shape[0]

    def x_copy(tile):
        return pltpu.make_async_copy(x_hbm.at[pl.ds(tile * tm, tm), :], xbuf_ref, sem)

    @pl.when((i == 0) & (j == 0))
    def _():
        x_copy(0).start()

    @pl.when(j == 0)
    def _():
        x_copy(i).wait()
        for r in range(tm // row_chunk):
            rows = pl.ds(r * row_chunk, row_chunk)
            x = xbuf_ref[rows, :]
            h_ref[rows, :] = _rms_rows(x, gain_ref[...]).astype(BF16)
            o_ref[rows, :] = x

    @pl.when((j == 1) & (i + 1 < n_tiles))
    def _():
        x_copy(i + 1).start()

    for src, dst in zip(cast_src, cast_dst):
        dst[...] = src[...].astype(BF16)

    h = h_ref[...]
    g = jnp.dot(h, wg_ref[...].astype(BF16), preferred_element_type=F32)
    u = jnp.dot(h, wu_ref[...].astype(BF16), preferred_element_type=F32)
    a = (0.5 * (g * jax.nn.sigmoid(g)) * u).astype(BF16)
    o_ref[...] += jnp.dot(a, wd_ref[...].astype(BF16), preferred_element_type=F32)


def _ffn(x, gain, wg, wu, wd, cast=(), *, tm=1024, tf=512, row_chunk=128):
    s, d = x.shape
    d_ff = wg.shape[1]
    n_i, n_j = s // tm, d_ff // tf
    cast_specs = []
    for w in cast:
        n_blocks = max(n for n in range(1, n_i * n_j + 1)
                       if w.shape[0] % n == 0 and (w.shape[0] // n) % 16 == 0)
        index = lambda i, j, n_blocks=n_blocks: (jnp.minimum(i * n_j + j, n_blocks - 1), 0)
        cast_specs.append(pl.BlockSpec((w.shape[0] // n_blocks, w.shape[1]), index))
    outs = pl.pallas_call(
        functools.partial(_ffn_kernel, row_chunk=row_chunk, n_cast=len(cast)),
        name="ffn",
        grid=(n_i, n_j),
        in_specs=[
            pl.BlockSpec(memory_space=pl.ANY),
            pl.BlockSpec((1, d), lambda i, j: (0, 0)),
            pl.BlockSpec((d, tf), lambda i, j: (0, j)),
            pl.BlockSpec((d, tf), lambda i, j: (0, j)),
            pl.BlockSpec((tf, d), lambda i, j: (j, 0)),
            *cast_specs,
        ],
        out_specs=[pl.BlockSpec((tm, d), lambda i, j: (i, 0)), *cast_specs],
        out_shape=[jax.ShapeDtypeStruct((s, d), F32),
                   *[jax.ShapeDtypeStruct(w.shape, BF16) for w in cast]],
        scratch_shapes=[pltpu.VMEM((tm, d), BF16), pltpu.VMEM((tm, d), F32),
                        pltpu.SemaphoreType.DMA(())],
        compiler_params=pltpu.CompilerParams(
            dimension_semantics=("arbitrary", "arbitrary"),
            vmem_limit_bytes=VMEM_LIMIT_BYTES,
        ),
    )(x, gain, wg, wu, wd, *cast)
    return tuple(outs)


_SEGMENTS = (("q_r", RET_WIDTH, False), ("k_r", RET_WIDTH, False), ("v_r", RET_WIDTH, True),
             ("g_r", RET_WIDTH, False), ("q_a", ATT_WIDTH, False), ("k_a", KV_WIDTH, False),
             ("v_a", KV_WIDTH, True))


def _z_layout():
    layout, w_off, widths = {}, 0, {False: 0, True: 0}
    for name, width, is_value in _SEGMENTS:
        layout[name] = (is_value, widths[is_value], w_off)
        widths[is_value] += width
        w_off += width
    return layout, widths[False], widths[True]


_Z_LAYOUT, ZF_WIDTH, ZV_WIDTH = _z_layout()


def _z_dest(w_col):
    for name, width, _ in _SEGMENTS:
        is_value, z_off, w_off = _Z_LAYOUT[name]
        if w_off <= w_col < w_off + width:
            return is_value, z_off + (w_col - w_off)
    raise ValueError(w_col)


def _dot_nt(a, b):
    return lax.dot_general(a, b, (((1,), (1,)), ((), ())), preferred_element_type=F32)


def _dot_tn(a, b):
    return lax.dot_general(a, b, (((0,), (0,)), ((), ())), preferred_element_type=F32)


LANE_SIGN_R, LANE_MASK_LO, LANE_MASK_HI = 0, 1, 2
OFF_COS_R, OFF_SIN_R, OFF_COS_A, OFF_SIN_A = 0, 1, 2, 3
GAM_LOG, GAM_CHUNK = 0, 1
RET_GROUP = 4


def _mix_chunk(zhead, rows, not_first, sink_ref, gam_ref, base_refs, off, lane, qn_ref, kn_ref,
               y_ref, state_ref, kprev_ref, vprev_ref):
    cb_r_ref, sb_r_ref, cb_a_ref, sb_a_ref = base_refs
    row = lambda t, k: t[k:k + 1, :]

    cb, sb = cb_r_ref[rows, :], sb_r_ref[rows, :]
    co, so = row(off, OFF_COS_R), row(off, OFF_SIN_R)
    cos_r = cb * co - sb * so
    sin_r = (sb * co + cb * so) * row(lane, LANE_SIGN_R)

    def rot_r(t):
        return t * cos_r + pltpu.roll(t, HEAD_DIM // 2, 1) * sin_r

    n = lax.broadcasted_iota(jnp.int32, (CHUNK, CHUNK), 0)
    m = lax.broadcasted_iota(jnp.int32, (CHUNK, CHUNK), 1)
    causal = n >= m
    lag = jnp.maximum(n - m, 0).astype(F32)
    n_plus_1 = (n + 1).astype(F32)
    to_end = (CHUNK - 1 - n).astype(F32)

    for h0 in range(0, RET_HEADS, RET_GROUP):
        hs = range(h0, h0 + RET_GROUP)
        log_g = [gam_ref[GAM_LOG, h] for h in hs]
        q = [rot_r(zhead("q_r", h)) for h in hs]
        k = [rot_r(zhead("k_r", h)) * (HEAD_DIM ** -0.5) for h in hs]
        vb = [zhead("v_r", h) for h in hs]
        scores = [_dot_nt(q[i].astype(BF16), k[i].astype(BF16))
                  * jnp.where(causal, jnp.exp(lag * log_g[i]), 0.0) for i in range(RET_GROUP)]
        prev = [state_ref[h] for h in hs]
        y = [jnp.dot(jnp.concatenate([scores[i].astype(BF16),
                                      (q[i] * jnp.exp(n_plus_1 * log_g[i])).astype(BF16)], axis=1),
                     jnp.concatenate([vb[i], prev[i].astype(BF16)], axis=0),
                     preferred_element_type=F32) for i in range(RET_GROUP)]
        chunk_kv = [_dot_tn((k[i] * jnp.exp(to_end * log_g[i])).astype(BF16), vb[i])
                    for i in range(RET_GROUP)]
        for i, h in enumerate(hs):
            state_ref[h] = prev[i] * gam_ref[GAM_CHUNK, h] + chunk_kv[i]
        y = [y[i] * lax.rsqrt(jnp.mean(y[i] * y[i], axis=-1, keepdims=True) + EPS)
             for i in range(RET_GROUP)]
        for i, h in enumerate(hs):
            g = zhead("g_r", h)
            y_ref[rows, h * HEAD_DIM:(h + 1) * HEAD_DIM] = (
                (g * jax.nn.sigmoid(g)) * y[i]).astype(BF16)
        yield

    cb, sb = cb_a_ref[rows, :], sb_a_ref[rows, :]
    co, so = row(off, OFF_COS_A), row(off, OFF_SIN_A)
    cos_a = cb * co - sb * so
    sin_a = sb * co + cb * so
    sin_lo = sin_a * row(lane, LANE_MASK_LO)
    sin_hi = sin_a * row(lane, LANE_MASK_HI)
    half = ROPE_DIM // 2

    def rot_a(t):
        return (t * cos_a + pltpu.roll(t, HEAD_DIM - half, 1) * sin_lo
                + pltpu.roll(t, half, 1) * sin_hi)

    qi = lax.broadcasted_iota(jnp.int32, (WINDOW, 2 * WINDOW), 0)
    kj = lax.broadcasted_iota(jnp.int32, (WINDOW, 2 * WINDOW), 1)
    rel = WINDOW + qi - kj
    mask = (rel >= 0) & (rel < WINDOW) & ((kj >= WINDOW) | not_first)
    neg = jnp.finfo(F32).min

    for kh in range(ATT_KV_HEADS):
        kb = rot_a(_rms_rows(zhead("k_a", kh), kn_ref[...])).astype(BF16)
        vb = zhead("v_a", kh)
        kk = jnp.concatenate([kprev_ref[kh], kb], axis=0)
        vv = jnp.concatenate([vprev_ref[kh], vb], axis=0)
        heads = range(kh * GQA_GROUP, (kh + 1) * GQA_GROUP)
        qs = jnp.concatenate(
            [rot_a(_rms_rows(zhead("q_a", qh), qn_ref[...])).astype(BF16) for qh in heads], axis=0)
        s_all = _dot_nt(qs, kk) * (HEAD_DIM ** -0.5)
        sinks = [sink_ref[qh] for qh in heads]
        s = [jnp.where(mask, s_all[gq * WINDOW:(gq + 1) * WINDOW, :], neg)
             for gq in range(GQA_GROUP)]
        mx = [jnp.maximum(jnp.max(s[gq], axis=-1, keepdims=True), sinks[gq])
              for gq in range(GQA_GROUP)]
        p = [jnp.exp(s[gq] - mx[gq]) for gq in range(GQA_GROUP)]
        inv_denoms = [1.0 / (jnp.sum(p[gq], axis=-1, keepdims=True) + jnp.exp(sinks[gq] - mx[gq]))
                      for gq in range(GQA_GROUP)]
        probs = [p[gq].astype(BF16) for gq in range(GQA_GROUP)]
        o_all = jnp.dot(jnp.concatenate(probs, axis=0), vv, preferred_element_type=F32)
        for gq, qh in enumerate(heads):
            o = o_all[gq * WINDOW:(gq + 1) * WINDOW, :] * inv_denoms[gq]
            col = RET_WIDTH + qh * HEAD_DIM
            y_ref[rows, col:col + HEAD_DIM] = o.astype(BF16)
        kprev_ref[kh] = kb
        vprev_ref[kh] = vb
        yield


def _mix_layer_kernel(sink_ref, gam_ref, xn_ref, gain_ref, win_ref, cb_r_ref, sb_r_ref,
                      cb_a_ref, sb_a_ref, off_ref, lane_ref, qn_ref, kn_ref, wout_ref, o_ref,
                      zfa_ref, zva_ref, zfb_ref, zvb_ref, h_ref, xprev_ref, y_ref, state_ref,
                      kprev_ref, vprev_ref, *, proj_tile):
    s = pl.program_id(0)
    last = pl.num_programs(0) - 1
    rb = xn_ref.shape[0]
    base_refs = (cb_r_ref, sb_r_ref, cb_a_ref, sb_a_ref)
    z_a = {False: zfa_ref, True: zva_ref}
    z_b = {False: zfb_ref, True: zvb_ref}

    def step(project_into, mix_from):
        pieces = iter(())
        if mix_from is not None:
            off = off_ref[0]
            lane = lane_ref[...]

            def chunk_pieces(cc):
                rows = slice(cc * CHUNK, (cc + 1) * CHUNK)

                def zhead(name, hd):
                    is_value, z_off, _ = _Z_LAYOUT[name]
                    return mix_from[is_value][rows,
                                              z_off + hd * HEAD_DIM: z_off + (hd + 1) * HEAD_DIM]

                not_first = (s > 1) if cc == 0 else True
                return _mix_chunk(zhead, rows, not_first, sink_ref, gam_ref, base_refs, off,
                                  lane, qn_ref, kn_ref, y_ref, state_ref, kprev_ref, vprev_ref)

            pieces = itertools.chain(*[chunk_pieces(cc) for cc in range(rb // CHUNK)])

        if project_into is not None:
            n_tiles = IN_WIDTH // proj_tile
            n_pieces = (rb // CHUNK) * (RET_HEADS // RET_GROUP + ATT_KV_HEADS)
            h_ref[...] = _rms_rows(xn_ref[...], gain_ref[...]).astype(BF16)
            for t in range(n_tiles):
                w_col = t * proj_tile
                is_value, z_col = _z_dest(w_col)
                dst = project_into[is_value]
                tile = jnp.dot(h_ref[...], win_ref[:, w_col:w_col + proj_tile],
                               preferred_element_type=F32)
                dst[:, z_col:z_col + proj_tile] = tile.astype(dst.dtype)
                if (t + 1) * n_pieces // n_tiles > t * n_pieces // n_tiles:
                    next(pieces, None)
        for _ in pieces:
            pass

        if mix_from is not None:
            o_ref[...] = xprev_ref[...] + jnp.dot(y_ref[...], wout_ref[...],
                                                  preferred_element_type=F32)
        if project_into is not None:
            xprev_ref[...] = xn_ref[...]

    @pl.when(s == 0)
    def _():
        state_ref[...] = jnp.zeros_like(state_ref)
        kprev_ref[...] = jnp.zeros_like(kprev_ref)
        vprev_ref[...] = jnp.zeros_like(vprev_ref)
        step(z_b, None)

    @pl.when((s > 0) & (s < last))
    def _():
        step(z_a, z_b)
        for is_value in (False, True):
            z_b[is_value][...] = z_a[is_value][...]

    @pl.when(s == last)
    def _():
        step(None, z_b)


def _mix_layer(x, gain, w_in, w_out, sinks, q_gain, k_gain, *, rb=256, proj_tile=256):
    s, d = x.shape
    nb = s // rb
    t = _position_tables(rb, nb)
    cur = lambda i: (jnp.minimum(i, nb - 1), 0)
    prv = lambda i: (jnp.maximum(i - 1, 0), 0)
    const2 = lambda i: (0, 0)
    resident = dict(pipeline_mode=pl.Buffered(1))
    base_tab = pl.BlockSpec((rb, HEAD_DIM), const2, **resident)
    smem = pl.BlockSpec(memory_space=pltpu.SMEM)
    return pl.pallas_call(
        functools.partial(_mix_layer_kernel, proj_tile=proj_tile),
        name="mix_layer",
        grid=(nb + 1,),
        in_specs=[
            smem, smem,
            pl.BlockSpec((rb, d), cur),
            pl.BlockSpec((1, d), const2),
            pl.BlockSpec(w_in.shape, const2, **resident),
            base_tab, base_tab, base_tab, base_tab,
            pl.BlockSpec((1,) + t["block_off"].shape[1:], lambda i: (jnp.maximum(i - 1, 0), 0, 0)),
            pl.BlockSpec(t["lane"].shape, const2),
            pl.BlockSpec((1, HEAD_DIM), const2), pl.BlockSpec((1, HEAD_DIM), const2),
            pl.BlockSpec(w_out.shape, const2, **resident),
        ],
        out_specs=pl.BlockSpec((rb, d), prv),
        out_shape=jax.ShapeDtypeStruct((s, d), F32),
        scratch_shapes=[
            pltpu.VMEM((rb, ZF_WIDTH), F32), pltpu.VMEM((rb, ZV_WIDTH), BF16),
            pltpu.VMEM((rb, ZF_WIDTH), F32), pltpu.VMEM((rb, ZV_WIDTH), BF16),
            pltpu.VMEM((rb, d), BF16),
            pltpu.VMEM((rb, d), F32),
            pltpu.VMEM((rb, MIX_WIDTH), BF16),
            pltpu.VMEM((RET_HEADS, HEAD_DIM, HEAD_DIM), F32),
            pltpu.VMEM((ATT_KV_HEADS, WINDOW, HEAD_DIM), BF16),
            pltpu.VMEM((ATT_KV_HEADS, WINDOW, HEAD_DIM), BF16),
        ],
        compiler_params=pltpu.CompilerParams(
            dimension_semantics=("arbitrary",),
            vmem_limit_bytes=VMEM_LIMIT_BYTES,
        ),
    )(sinks, t["gamma"], x, gain, w_in, t["cos_base_r"], t["sin_base_r"], t["cos_base_a"],
      t["sin_base_a"], t["block_off"], t["lane"], q_gain, k_gain, w_out)


def _position_tables(rb, nb):
    f32 = lambda a: jnp.asarray(np.asarray(a, np.float32))
    r = np.arange(rb, dtype=np.float64)[:, None]
    start = (np.arange(nb, dtype=np.float64) * rb)[:, None]

    inv_r = RET_THETA ** (-np.arange(0, HEAD_DIM, 2, dtype=np.float64) / HEAD_DIM)
    inv_r = np.concatenate([inv_r, inv_r])
    inv_a = ROPE_THETA ** (-np.arange(0, ROPE_DIM, 2, dtype=np.float64) / ROPE_DIM)
    inv_a = np.concatenate([inv_a, inv_a, np.zeros(HEAD_DIM - ROPE_DIM)])

    block_off = np.stack([np.cos(start * inv_r), np.sin(start * inv_r),
                          np.cos(start * inv_a), np.sin(start * inv_a)], axis=1)

    half = ROPE_DIM // 2
    lane = np.zeros((8, HEAD_DIM))
    lane[LANE_SIGN_R] = np.where(np.arange(HEAD_DIM) < HEAD_DIM // 2, -1.0, 1.0)
    lane[LANE_MASK_LO, :half] = -1.0
    lane[LANE_MASK_HI, half:ROPE_DIM] = 1.0

    log_g = np.log1p(-np.exp2(-5.0 - np.arange(RET_HEADS, dtype=np.float64)))
    gamma = np.stack([log_g, np.exp(CHUNK * log_g)])
    return {
        "cos_base_r": f32(np.cos(r * inv_r)), "sin_base_r": f32(np.sin(r * inv_r)),
        "cos_base_a": f32(np.cos(r * inv_a)), "sin_base_a": f32(np.sin(r * inv_a)),
        "block_off": f32(block_off), "lane": f32(lane), "gamma": f32(gamma),
    }


def kernel(x, ffn1_norm, ffn1_w_gate, ffn1_w_up, ffn1_w_down, mix_norm, w_in, q_norm, k_norm,
           attn_sinks, w_out, ffn2_norm, ffn2_w_gate, ffn2_w_up, ffn2_w_down):
    b, s, d = x.shape
    depth = ffn1_norm.shape[0]
    outs = []
    for bi in range(b):
        xb = x[bi]
        for l in range(depth):
            xb, w_in_bf16, w_out_bf16 = _ffn(
                xb, ffn1_norm[l][None], ffn1_w_gate[l], ffn1_w_up[l], ffn1_w_down[l],
                cast=(w_in[l], w_out[l]))
            xb = _mix_layer(xb, mix_norm[l][None], w_in_bf16, w_out_bf16, attn_sinks[l],
                            q_norm[l][None], k_norm[l][None])
            xb, = _ffn(xb, ffn2_norm[l][None], ffn2_w_gate[l], ffn2_w_up[l], ffn2_w_down[l])
        outs.append(xb)
    return jnp.stack(outs, axis=0)
```

```python
import functools
import itertools

import jax
import jax.numpy as jnp
import numpy as np
from jax import lax
from jax.experimental import pallas as pl
from jax.experimental.pallas import tpu as pltpu

D_MODEL = 2048
HEAD_DIM = 128
RET_HEADS = 8
ATT_Q_HEADS = 8
ATT_KV_HEADS = 2
GQA_GROUP = ATT_Q_HEADS // ATT_KV_HEADS
RET_WIDTH = RET_HEADS * HEAD_DIM
ATT_WIDTH = ATT_Q_HEADS * HEAD_DIM
KV_WIDTH = ATT_KV_HEADS * HEAD_DIM
MIX_WIDTH = RET_WIDTH + ATT_WIDTH
IN_WIDTH = 4 * RET_WIDTH + ATT_WIDTH + 2 * KV_WIDTH
CHUNK = 128
WINDOW = 128
ROPE_THETA = 500000.0
ROPE_DIM = HEAD_DIM // 4
RET_THETA = 10000.0
EPS = 1e-6

VMEM_LIMIT_BYTES = 62 * 1024 * 1024

F32 = jnp.float32
BF16 = jnp.bfloat16


def _rms_rows(x, gain):
    ms = jnp.mean(x * x, axis=-1, keepdims=True)
    return x * lax.rsqrt(ms + EPS) * gain


def _ffn_kernel(*refs, row_chunk, n_cast):
    x_hbm, gain_ref, wg_ref, wu_ref, wd_ref = refs[:5]
    cast_src = refs[5:5 + n_cast]
    o_ref = refs[5 + n_cast]
    cast_dst = refs[6 + n_cast:6 + 2 * n_cast]
    h_ref, xbuf_ref, sem = refs[6 + 2 * n_cast:]
    i = pl.program_id(0)
    j = pl.program_id(1)
    n_tiles = pl.num_programs(0)
    tm = o_ref.shape[0]

    def x_copy(tile):
        return pltpu.make_async_copy(x_hbm.at[pl.ds(tile * tm, tm), :], xbuf_ref, sem)

    @pl.when((i == 0) & (j == 0))
    def _():
        x_copy(0).start()

    @pl.when(j == 0)
    def _():
        x_copy(i).wait()
        for r in range(tm // row_chunk):
            rows = pl.ds(r * row_chunk, row_chunk)
            x = xbuf_ref[rows, :]
            h_ref[rows, :] = _rms_rows(x, gain_ref[...]).astype(BF16)
            o_ref[rows, :] = x

    @pl.when((j == 1) & (i + 1 < n_tiles))
    def _():
        x_copy(i + 1).start()

    for src, dst in zip(cast_src, cast_dst):
        dst[...] = src[...].astype(BF16)

    h = h_ref[...]
    g = jnp.dot(h, wg_ref[...].astype(BF16), preferred_element_type=F32)
    u = jnp.dot(h, wu_ref[...].astype(BF16), preferred_element_type=F32)
    a = (0.5 * (g * jax.nn.sigmoid(g)) * u).astype(BF16)
    o_ref[...] += jnp.dot(a, wd_ref[...].astype(BF16), preferred_element_type=F32)


def _ffn(x, gain, wg, wu, wd, cast=(), *, tm=1024, tf=512, row_chunk=128):
    s, d = x.shape
    d_ff = wg.shape[1]
    n_i, n_j = s // tm, d_ff // tf
    cast_specs = []
    for w in cast:
        n_blocks = max(n for n in range(1, n_i * n_j + 1)
                       if w.shape[0] % n == 0 and (w.shape[0] // n) % 16 == 0)
        index = lambda i, j, n_blocks=n_blocks: (jnp.minimum(i * n_j + j, n_blocks - 1), 0)
        cast_specs.append(pl.BlockSpec((w.shape[0] // n_blocks, w.shape[1]), index))
    outs = pl.pallas_call(
        functools.partial(_ffn_kernel, row_chunk=row_chunk, n_cast=len(cast)),
        name="ffn",
        grid=(n_i, n_j),
        in_specs=[
            pl.BlockSpec(memory_space=pl.ANY),
            pl.BlockSpec((1, d), lambda i, j: (0, 0)),
            pl.BlockSpec((d, tf), lambda i, j: (0, j)),
            pl.BlockSpec((d, tf), lambda i, j: (0, j)),
            pl.BlockSpec((tf, d), lambda i, j: (j, 0)),
            *cast_specs,
        ],
        out_specs=[pl.BlockSpec((tm, d), lambda i, j: (i, 0)), *cast_specs],
        out_shape=[jax.ShapeDtypeStruct((s, d), F32),
                   *[jax.ShapeDtypeStruct(w.shape, BF16) for w in cast]],
        scratch_shapes=[pltpu.VMEM((tm, d), BF16), pltpu.VMEM((tm, d), F32),
                        pltpu.SemaphoreType.DMA(())],
        compiler_params=pltpu.CompilerParams(
            dimension_semantics=("arbitrary", "arbitrary"),
            vmem_limit_bytes=VMEM_LIMIT_BYTES,
        ),
    )(x, gain, wg, wu, wd, *cast)
    return tuple(outs)


_SEGMENTS = (("q_r", RET_WIDTH, False), ("k_r", RET_WIDTH, False), ("v_r", RET_WIDTH, True),
             ("g_r", RET_WIDTH, False), ("q_a", ATT_WIDTH, False), ("k_a", KV_WIDTH, False),
             ("v_a", KV_WIDTH, True))


def _z_layout():
    layout, w_off, widths = {}, 0, {False: 0, True: 0}
    for name, width, is_value in _SEGMENTS:
        layout[name] = (is_value, widths[is_value], w_off)
        widths[is_value] += width
        w_off += width
    return layout, widths[False], widths[True]


_Z_LAYOUT, ZF_WIDTH, ZV_WIDTH = _z_layout()


def _z_dest(w_col):
    for name, width, _ in _SEGMENTS:
        is_value, z_off, w_off = _Z_LAYOUT[name]
        if w_off <= w_col < w_off + width:
            return is_value, z_off + (w_col - w_off)
    raise ValueError(w_col)


def _dot_nt(a, b):
    return lax.dot_general(a, b, (((1,), (1,)), ((), ())), preferred_element_type=F32)


def _dot_tn(a, b):
    return lax.dot_general(a, b, (((0,), (0,)), ((), ())), preferred_element_type=F32)


LANE_SIGN_R, LANE_MASK_LO, LANE_MASK_HI = 0, 1, 2
OFF_COS_R, OFF_SIN_R, OFF_COS_A, OFF_SIN_A = 0, 1, 2, 3
GAM_LOG, GAM_CHUNK = 0, 1
RET_GROUP = 8


def _mix_chunk(zhead, rows, not_first, sink_ref, gam_ref, base_refs, off, lane, qn_ref, kn_ref,
               y_ref, state_ref, kprev_ref, vprev_ref):
    cb_r_ref, sb_r_ref, cb_a_ref, sb_a_ref = base_refs
    row = lambda t, k: t[k:k + 1, :]

    cb, sb = cb_r_ref[rows, :], sb_r_ref[rows, :]
    co, so = row(off, OFF_COS_R), row(off, OFF_SIN_R)
    cos_r = cb * co - sb * so
    sin_r = (sb * co + cb * so) * row(lane, LANE_SIGN_R)

    def rot_r(t):
        return t * cos_r + pltpu.roll(t, HEAD_DIM // 2, 1) * sin_r

    n = lax.broadcasted_iota(jnp.int32, (CHUNK, CHUNK), 0)
    m = lax.broadcasted_iota(jnp.int32, (CHUNK, CHUNK), 1)
    causal = n >= m
    lag = jnp.maximum(n - m, 0).astype(F32)
    n_plus_1 = (n + 1).astype(F32)
    to_end = (CHUNK - 1 - n).astype(F32)

    for h0 in range(0, RET_HEADS, RET_GROUP):
        hs = range(h0, h0 + RET_GROUP)
        log_g = [gam_ref[GAM_LOG, h] for h in hs]
        q = [rot_r(zhead("q_r", h)) for h in hs]
        k = [rot_r(zhead("k_r", h)) * (HEAD_DIM ** -0.5) for h in hs]
        vb = [zhead("v_r", h) for h in hs]
        scores = [_dot_nt(q[i].astype(BF16), k[i].astype(BF16))
                  * jnp.where(causal, jnp.exp(lag * log_g[i]), 0.0) for i in range(RET_GROUP)]
        prev = [state_ref[h] for h in hs]
        y = [jnp.dot(jnp.concatenate([scores[i].astype(BF16),
                                      (q[i] * jnp.exp(n_plus_1 * log_g[i])).astype(BF16)], axis=1),
                     jnp.concatenate([vb[i], prev[i].astype(BF16)], axis=0),
                     preferred_element_type=F32) for i in range(RET_GROUP)]
        chunk_kv = [_dot_tn((k[i] * jnp.exp(to_end * log_g[i])).astype(BF16), vb[i])
                    for i in range(RET_GROUP)]
        for i, h in enumerate(hs):
            state_ref[h] = prev[i] * gam_ref[GAM_CHUNK, h] + chunk_kv[i]
        y = [y[i] * lax.rsqrt(jnp.mean(y[i] * y[i], axis=-1, keepdims=True) + EPS)
             for i in range(RET_GROUP)]
        for i, h in enumerate(hs):
            g = zhead("g_r", h)
            y_ref[rows, h * HEAD_DIM:(h + 1) * HEAD_DIM] = (
                (g * jax.nn.sigmoid(g)) * y[i]).astype(BF16)
        yield

    cb, sb = cb_a_ref[rows, :], sb_a_ref[rows, :]
    co, so = row(off, OFF_COS_A), row(off, OFF_SIN_A)
    cos_a = cb * co - sb * so
    sin_a = sb * co + cb * so
    sin_lo = sin_a * row(lane, LANE_MASK_LO)
    sin_hi = sin_a * row(lane, LANE_MASK_HI)
    half = ROPE_DIM // 2

    def rot_a(t):
        return (t * cos_a + pltpu.roll(t, HEAD_DIM - half, 1) * sin_lo
                + pltpu.roll(t, half, 1) * sin_hi)

    qi = lax.broadcasted_iota(jnp.int32, (WINDOW, 2 * WINDOW), 0)
    kj = lax.broadcasted_iota(jnp.int32, (WINDOW, 2 * WINDOW), 1)
    rel = WINDOW + qi - kj
    mask = (rel >= 0) & (rel < WINDOW) & ((kj >= WINDOW) | not_first)
    neg = jnp.finfo(F32).min

    kvs = range(ATT_KV_HEADS)
    all_heads = range(ATT_Q_HEADS)
    kb = [rot_a(_rms_rows(zhead("k_a", kh), kn_ref[...])).astype(BF16) for kh in kvs]
    vb = [zhead("v_a", kh) for kh in kvs]
    kk = [jnp.concatenate([kprev_ref[kh], kb[kh]], axis=0) for kh in kvs]
    vv = [jnp.concatenate([vprev_ref[kh], vb[kh]], axis=0) for kh in kvs]
    qb = [rot_a(_rms_rows(zhead("q_a", qh), qn_ref[...])).astype(BF16) for qh in all_heads]
    s_all = [_dot_nt(jnp.concatenate(qb[kh * GQA_GROUP:(kh + 1) * GQA_GROUP], axis=0), kk[kh])
             * (HEAD_DIM ** -0.5) for kh in kvs]
    sinks = [sink_ref[qh] for qh in all_heads]
    s = [jnp.where(mask, s_all[qh // GQA_GROUP][(qh % GQA_GROUP) * WINDOW:
                                                (qh % GQA_GROUP + 1) * WINDOW, :], neg)
         for qh in all_heads]
    mx = [jnp.maximum(jnp.max(s[qh], axis=-1, keepdims=True), sinks[qh]) for qh in all_heads]
    p = [jnp.exp(s[qh] - mx[qh]) for qh in all_heads]
    inv_denoms = [1.0 / (jnp.sum(p[qh], axis=-1, keepdims=True) + jnp.exp(sinks[qh] - mx[qh]))
                  for qh in all_heads]
    probs = [p[qh].astype(BF16) for qh in all_heads]
    o_all = [jnp.dot(jnp.concatenate(probs[kh * GQA_GROUP:(kh + 1) * GQA_GROUP], axis=0), vv[kh],
                     preferred_element_type=F32) for kh in kvs]
    for qh in all_heads:
        gq = qh % GQA_GROUP
        o = o_all[qh // GQA_GROUP][gq * WINDOW:(gq + 1) * WINDOW, :] * inv_denoms[qh]
        col = RET_WIDTH + qh * HEAD_DIM
        y_ref[rows, col:col + HEAD_DIM] = o.astype(BF16)
    for kh in kvs:
        kprev_ref[kh] = kb[kh]
        vprev_ref[kh] = vb[kh]
    yield


def _mix_layer_kernel(sink_ref, gam_ref, xn_ref, gain_ref, win_ref, cb_r_ref, sb_r_ref,
                      cb_a_ref, sb_a_ref, off_ref, lane_ref, qn_ref, kn_ref, wout_ref, o_ref,
                      zfa_ref, zva_ref, zfb_ref, zvb_ref, h_ref, xprev_ref, y_ref, state_ref,
                      kprev_ref, vprev_ref, *, proj_tile):
    s = pl.program_id(0)
    last = pl.num_programs(0) - 1
    rb = xn_ref.shape[0]
    base_refs = (cb_r_ref, sb_r_ref, cb_a_ref, sb_a_ref)
    z_a = {False: zfa_ref, True: zva_ref}
    z_b = {False: zfb_ref, True: zvb_ref}

    def step(project_into, mix_from):
        pieces = iter(())
        if mix_from is not None:
            off = off_ref[0]
            lane = lane_ref[...]

            def chunk_pieces(cc):
                rows = slice(cc * CHUNK, (cc + 1) * CHUNK)

                def zhead(name, hd):
                    is_value, z_off, _ = _Z_LAYOUT[name]
                    return mix_from[is_value][rows,
                                              z_off + hd * HEAD_DIM: z_off + (hd + 1) * HEAD_DIM]

                not_first = (s > 1) if cc == 0 else True
                return _mix_chunk(zhead, rows, not_first, sink_ref, gam_ref, base_refs, off,
                                  lane, qn_ref, kn_ref, y_ref, state_ref, kprev_ref, vprev_ref)

            pieces = itertools.chain(*[chunk_pieces(cc) for cc in range(rb // CHUNK)])

        if project_into is not None:
            n_tiles = IN_WIDTH // proj_tile
            n_pieces = (rb // CHUNK) * (RET_HEADS // RET_GROUP + 1)
            h_ref[...] = _rms_rows(xn_ref[...], gain_ref[...]).astype(BF16)
            for t in range(n_tiles):
                w_col = t * proj_tile
                is_value, z_col = _z_dest(w_col)
                dst = project_into[is_value]
                tile = jnp.dot(h_ref[...], win_ref[:, w_col:w_col + proj_tile],
                               preferred_element_type=F32)
                dst[:, z_col:z_col + proj_tile] = tile.astype(dst.dtype)
                if (t + 1) * n_pieces // n_tiles > t * n_pieces // n_tiles:
                    next(pieces, None)
        for _ in pieces:
            pass

        if mix_from is not None:
            o_ref[...] = xprev_ref[...] + jnp.dot(y_ref[...], wout_ref[...],
                                                  preferred_element_type=F32)
        if project_into is not None:
            xprev_ref[...] = xn_ref[...]

    @pl.when(s == 0)
    def _():
        state_ref[...] = jnp.zeros_like(state_ref)
        kprev_ref[...] = jnp.zeros_like(kprev_ref)
        vprev_ref[...] = jnp.zeros_like(vprev_ref)
        step(z_b, None)

    @pl.when((s > 0) & (s < last))
    def _():
        step(z_a, z_b)
        for is_value in (False, True):
            z_b[is_value][...] = z_a[is_value][...]

    @pl.when(s == last)
    def _():
        step(None, z_b)


def _mix_layer(x, gain, w_in, w_out, sinks, q_gain, k_gain, *, rb=256, proj_tile=256):
    s, d = x.shape
    nb = s // rb
    t = _position_tables(rb, nb)
    cur = lambda i: (jnp.minimum(i, nb - 1), 0)
    prv = lambda i: (jnp.maximum(i - 1, 0), 0)
    const2 = lambda i: (0, 0)
    resident = dict(pipeline_mode=pl.Buffered(1))
    base_tab = pl.BlockSpec((rb, HEAD_DIM), const2, **resident)
    smem = pl.BlockSpec(memory_space=pltpu.SMEM)
    return pl.pallas_call(
        functools.partial(_mix_layer_kernel, proj_tile=proj_tile),
        name="mix_layer",
        grid=(nb + 1,),
        in_specs=[
            smem, smem,
            pl.BlockSpec((rb, d), cur),
            pl.BlockSpec((1, d), const2),
            pl.BlockSpec(w_in.shape, const2, **resident),
            base_tab, base_tab, base_tab, base_tab,
            pl.BlockSpec((1,) + t["block_off"].shape[1:], lambda i: (jnp.maximum(i - 1, 0), 0, 0)),
            pl.BlockSpec(t["lane"].shape, const2),
            pl.BlockSpec((1, HEAD_DIM), const2), pl.BlockSpec((1, HEAD_DIM), const2),
            pl.BlockSpec(w_out.shape, const2, **resident),
        ],
        out_specs=pl.BlockSpec((rb, d), prv),
        out_shape=jax.ShapeDtypeStruct((s, d), F32),
        scratch_shapes=[
            pltpu.VMEM((rb, ZF_WIDTH), F32), pltpu.VMEM((rb, ZV_WIDTH), BF16),
            pltpu.VMEM((rb, ZF_WIDTH), F32), pltpu.VMEM((rb, ZV_WIDTH), BF16),
            pltpu.VMEM((rb, d), BF16),
            pltpu.VMEM((rb, d), F32),
            pltpu.VMEM((rb, MIX_WIDTH), BF16),
            pltpu.VMEM((RET_HEADS, HEAD_DIM, HEAD_DIM), F32),
            pltpu.VMEM((ATT_KV_HEADS, WINDOW, HEAD_DIM), BF16),
            pltpu.VMEM((ATT_KV_HEADS, WINDOW, HEAD_DIM), BF16),
        ],
        compiler_params=pltpu.CompilerParams(
            dimension_semantics=("arbitrary",),
            vmem_limit_bytes=VMEM_LIMIT_BYTES,
        ),
    )(sinks, t["gamma"], x, gain, w_in, t["cos_base_r"], t["sin_base_r"], t["cos_base_a"],
      t["sin_base_a"], t["block_off"], t["lane"], q_gain, k_gain, w_out)


def _position_tables(rb, nb):
    f32 = lambda a: jnp.asarray(np.asarray(a, np.float32))
    r = np.arange(rb, dtype=np.float64)[:, None]
    start = (np.arange(nb, dtype=np.float64) * rb)[:, None]

    inv_r = RET_THETA ** (-np.arange(0, HEAD_DIM, 2, dtype=np.float64) / HEAD_DIM)
    inv_r = np.concatenate([inv_r, inv_r])
    inv_a = ROPE_THETA ** (-np.arange(0, ROPE_DIM, 2, dtype=np.float64) / ROPE_DIM)
    inv_a = np.concatenate([inv_a, inv_a, np.zeros(HEAD_DIM - ROPE_DIM)])

    block_off = np.stack([np.cos(start * inv_r), np.sin(start * inv_r),
                          np.cos(start * inv_a), np.sin(start * inv_a)], axis=1)

    half = ROPE_DIM // 2
    lane = np.zeros((8, HEAD_DIM))
    lane[LANE_SIGN_R] = np.where(np.arange(HEAD_DIM) < HEAD_DIM // 2, -1.0, 1.0)
    lane[LANE_MASK_LO, :half] = -1.0
    lane[LANE_MASK_HI, half:ROPE_DIM] = 1.0

    log_g = np.log1p(-np.exp2(-5.0 - np.arange(RET_HEADS, dtype=np.float64)))
    gamma = np.stack([log_g, np.exp(CHUNK * log_g)])
    return {
        "cos_base_r": f32(np.cos(r * inv_r)), "sin_base_r": f32(np.sin(r * inv_r)),
        "cos_base_a": f32(np.cos(r * inv_a)), "sin_base_a": f32(np.sin(r * inv_a)),
        "block_off": f32(block_off), "lane": f32(lane), "gamma": f32(gamma),
    }


def kernel(x, ffn1_norm, ffn1_w_gate, ffn1_w_up, ffn1_w_down, mix_norm, w_in, q_norm, k_norm,
           attn_sinks, w_out, ffn2_norm, ffn2_w_gate, ffn2_w_up, ffn2_w_down):
    b, s, d = x.shape
    depth = ffn1_norm.shape[0]
    outs = []
    for bi in range(b):
        xb = x[bi]
        for l in range(depth):
            xb, w_in_bf16, w_out_bf16 = _ffn(
                xb, ffn1_norm[l][None], ffn1_w_gate[l], ffn1_w_up[l], ffn1_w_down[l],
                cast=(w_in[l], w_out[l]))
            xb = _mix_layer(xb, mix_norm[l][None], w_in_bf16, w_out_bf16, attn_sinks[l],
                            q_norm[l][None], k_norm[l][None])
            xb, = _ffn(xb, ffn2_norm[l][None], ffn2_w_gate[l], ffn2_w_up[l], ffn2_w_down[l])
        outs.append(xb)
    return jnp.stack(outs, axis=0)
```

```python
import functools
import itertools

import jax
import jax.numpy as jnp
import numpy as np
from jax import lax
from jax.experimental import pallas as pl
from jax.experimental.pallas import tpu as pltpu

D_MODEL = 2048
HEAD_DIM = 128
RET_HEADS = 8
ATT_Q_HEADS = 8
ATT_KV_HEADS = 2
GQA_GROUP = ATT_Q_HEADS // ATT_KV_HEADS
RET_WIDTH = RET_HEADS * HEAD_DIM
ATT_WIDTH = ATT_Q_HEADS * HEAD_DIM
KV_WIDTH = ATT_KV_HEADS * HEAD_DIM
MIX_WIDTH = RET_WIDTH + ATT_WIDTH
IN_WIDTH = 4 * RET_WIDTH + ATT_WIDTH + 2 * KV_WIDTH
CHUNK = 128
WINDOW = 128
ROPE_THETA = 500000.0
ROPE_DIM = HEAD_DIM // 4
RET_THETA = 10000.0
EPS = 1e-6

VMEM_LIMIT_BYTES = 62 * 1024 * 1024

F32 = jnp.float32
BF16 = jnp.bfloat16


def _rms_rows(x, gain):
    ms = jnp.mean(x * x, axis=-1, keepdims=True)
    return x * lax.rsqrt(ms + EPS) * gain


FFN_FIRST_FETCH_STEP = 2


def _ffn_kernel(*refs, row_chunk, n_cast):
    x_hbm, gain_ref, wg_ref, wu_ref, wd_ref = refs[:5]
    cast_src = refs[5:5 + n_cast]
    o_ref = refs[5 + n_cast]
    cast_dst = refs[6 + n_cast:6 + 2 * n_cast]
    h_ref, xbuf_ref, sem = refs[6 + 2 * n_cast:]
    i = pl.program_id(0)
    j = pl.program_id(1)
    n_tiles = pl.num_programs(0)
    tm = o_ref.shape[0]

    n_chunks = tm // row_chunk
    first_fetch_step = FFN_FIRST_FETCH_STEP

    def x_copy(tile, r):
        r0 = r * row_chunk
        if not isinstance(r0, int):
            r0 = pl.multiple_of(r0, row_chunk)
        return pltpu.make_async_copy(x_hbm.at[pl.ds(tile * tm + r0, row_chunk), :],
                                     xbuf_ref.at[pl.ds(r0, row_chunk), :], sem)

    @pl.when((i == 0) & (j == 0))
    def _():
        for r in range(n_chunks):
            x_copy(0, r).start()

    @pl.when(j == 0)
    def _():
        for r in range(n_chunks):
            x_copy(i, r).wait()
        for r in range(n_chunks):
            rows = pl.ds(r * row_chunk, row_chunk)
            x = xbuf_ref[rows, :]
            h_ref[rows, :] = _rms_rows(x, gain_ref[...]).astype(BF16)
            o_ref[rows, :] = x

    @pl.when((j >= first_fetch_step) & (j < first_fetch_step + n_chunks) & (i + 1 < n_tiles))
    def _():
        x_copy(i + 1, j - first_fetch_step).start()

    for src, dst in zip(cast_src, cast_dst):
        dst[...] = src[...].astype(BF16)

    h = h_ref[...]
    g = jnp.dot(h, wg_ref[...].astype(BF16), preferred_element_type=F32)
    u = jnp.dot(h, wu_ref[...].astype(BF16), preferred_element_type=F32)
    a = (0.5 * (g * jax.nn.sigmoid(g)) * u).astype(BF16)
    o_ref[...] += jnp.dot(a, wd_ref[...].astype(BF16), preferred_element_type=F32)


def _ffn(x, gain, wg, wu, wd, cast=(), *, tm=1024, tf=512, row_chunk=128):
    s, d = x.shape
    d_ff = wg.shape[1]
    n_i, n_j = s // tm, d_ff // tf
    assert FFN_FIRST_FETCH_STEP + tm // row_chunk <= n_j
    cast_specs = []
    for w in cast:
        n_blocks = max(n for n in range(1, n_i * n_j + 1)
                       if w.shape[0] % n == 0 and (w.shape[0] // n) % 16 == 0)
        index = lambda i, j, n_blocks=n_blocks: (jnp.minimum(i * n_j + j, n_blocks - 1), 0)
        cast_specs.append(pl.BlockSpec((w.shape[0] // n_blocks, w.shape[1]), index))
    outs = pl.pallas_call(
        functools.partial(_ffn_kernel, row_chunk=row_chunk, n_cast=len(cast)),
        name="ffn",
        grid=(n_i, n_j),
        in_specs=[
            pl.BlockSpec(memory_space=pl.ANY),
            pl.BlockSpec((1, d), lambda i, j: (0, 0)),
            pl.BlockSpec((d, tf), lambda i, j: (0, j)),
            pl.BlockSpec((d, tf), lambda i, j: (0, j)),
            pl.BlockSpec((tf, d), lambda i, j: (j, 0)),
            *cast_specs,
        ],
        out_specs=[pl.BlockSpec((tm, d), lambda i, j: (i, 0)), *cast_specs],
        out_shape=[jax.ShapeDtypeStruct((s, d), F32),
                   *[jax.ShapeDtypeStruct(w.shape, BF16) for w in cast]],
        scratch_shapes=[pltpu.VMEM((tm, d), BF16), pltpu.VMEM((tm, d), F32),
                        pltpu.SemaphoreType.DMA(())],
        compiler_params=pltpu.CompilerParams(
            dimension_semantics=("arbitrary", "arbitrary"),
            vmem_limit_bytes=VMEM_LIMIT_BYTES,
        ),
    )(x, gain, wg, wu, wd, *cast)
    return tuple(outs)


_SEGMENTS = (("q_r", RET_WIDTH, False), ("k_r", RET_WIDTH, False), ("v_r", RET_WIDTH, True),
             ("g_r", RET_WIDTH, False), ("q_a", ATT_WIDTH, False), ("k_a", KV_WIDTH, False),
             ("v_a", KV_WIDTH, True))


def _z_layout():
    layout, w_off, widths = {}, 0, {False: 0, True: 0}
    for name, width, is_value in _SEGMENTS:
        layout[name] = (is_value, widths[is_value], w_off)
        widths[is_value] += width
        w_off += width
    return layout, widths[False], widths[True]


_Z_LAYOUT, ZF_WIDTH, ZV_WIDTH = _z_layout()


def _z_dest(w_col):
    for name, width, _ in _SEGMENTS:
        is_value, z_off, w_off = _Z_LAYOUT[name]
        if w_off <= w_col < w_off + width:
            return is_value, z_off + (w_col - w_off)
    raise ValueError(w_col)


def _dot_nt(a, b):
    return lax.dot_general(a, b, (((1,), (1,)), ((), ())), preferred_element_type=F32)


def _dot_tn(a, b):
    return lax.dot_general(a, b, (((0,), (0,)), ((), ())), preferred_element_type=F32)


LANE_SIGN_R, LANE_MASK_LO, LANE_MASK_HI = 0, 1, 2
OFF_COS_R, OFF_SIN_R, OFF_COS_A, OFF_SIN_A = 0, 1, 2, 3
GAM_LOG, GAM_CHUNK = 0, 1
RET_GROUP = 8


def _mix_chunk(zhead, rows, not_first, sink_ref, gam_ref, base_refs, off, lane, qn_ref, kn_ref,
               y_ref, state_ref, kprev_ref, vprev_ref):
    cb_r_ref, sb_r_ref, cb_a_ref, sb_a_ref = base_refs
    row = lambda t, k: t[k:k + 1, :]

    cb, sb = cb_r_ref[rows, :], sb_r_ref[rows, :]
    co, so = row(off, OFF_COS_R), row(off, OFF_SIN_R)
    cos_r = cb * co - sb * so
    sin_r = (sb * co + cb * so) * row(lane, LANE_SIGN_R)

    def rot_r(t):
        return t * cos_r + pltpu.roll(t, HEAD_DIM // 2, 1) * sin_r

    n = lax.broadcasted_iota(jnp.int32, (CHUNK, CHUNK), 0)
    m = lax.broadcasted_iota(jnp.int32, (CHUNK, CHUNK), 1)
    causal = n >= m
    lag = jnp.maximum(n - m, 0).astype(F32)
    n_plus_1 = (n + 1).astype(F32)
    to_end = (CHUNK - 1 - n).astype(F32)

    for h0 in range(0, RET_HEADS, RET_GROUP):
        hs = range(h0, h0 + RET_GROUP)
        log_g = [gam_ref[GAM_LOG, h] for h in hs]
        q = [rot_r(zhead("q_r", h)) for h in hs]
        k = [rot_r(zhead("k_r", h)) * (HEAD_DIM ** -0.5) for h in hs]
        vb = [zhead("v_r", h) for h in hs]
        scores = [_dot_nt(q[i].astype(BF16), k[i].astype(BF16))
                  * jnp.where(causal, jnp.exp(lag * log_g[i]), 0.0) for i in range(RET_GROUP)]
        prev = [state_ref[h] for h in hs]
        y = [jnp.dot(jnp.concatenate([scores[i].astype(BF16),
                                      (q[i] * jnp.exp(n_plus_1 * log_g[i])).astype(BF16)], axis=1),
                     jnp.concatenate([vb[i], prev[i].astype(BF16)], axis=0),
                     preferred_element_type=F32) for i in range(RET_GROUP)]
        chunk_kv = [_dot_tn((k[i] * jnp.exp(to_end * log_g[i])).astype(BF16), vb[i])
                    for i in range(RET_GROUP)]
        for i, h in enumerate(hs):
            state_ref[h] = prev[i] * gam_ref[GAM_CHUNK, h] + chunk_kv[i]
        y = [y[i] * lax.rsqrt(jnp.mean(y[i] * y[i], axis=-1, keepdims=True) + EPS)
             for i in range(RET_GROUP)]
        for i, h in enumerate(hs):
            g = zhead("g_r", h)
            y_ref[rows, h * HEAD_DIM:(h + 1) * HEAD_DIM] = (
                (g * jax.nn.sigmoid(g)) * y[i]).astype(BF16)
        yield

    cb, sb = cb_a_ref[rows, :], sb_a_ref[rows, :]
    co, so = row(off, OFF_COS_A), row(off, OFF_SIN_A)
    cos_a = cb * co - sb * so
    sin_a = sb * co + cb * so
    sin_lo = sin_a * row(lane, LANE_MASK_LO)
    sin_hi = sin_a * row(lane, LANE_MASK_HI)
    half = ROPE_DIM // 2

    def rot_a(t):
        return (t * cos_a + pltpu.roll(t, HEAD_DIM - half, 1) * sin_lo
                + pltpu.roll(t, half, 1) * sin_hi)

    qi = lax.broadcasted_iota(jnp.int32, (WINDOW, 2 * WINDOW), 0)
    kj = lax.broadcasted_iota(jnp.int32, (WINDOW, 2 * WINDOW), 1)
    rel = WINDOW + qi - kj
    mask = (rel >= 0) & (rel < WINDOW) & ((kj >= WINDOW) | not_first)
    neg = jnp.finfo(F32).min

    kvs = range(ATT_KV_HEADS)
    all_heads = range(ATT_Q_HEADS)
    kb = [rot_a(_rms_rows(zhead("k_a", kh), kn_ref[...])).astype(BF16) for kh in kvs]
    vb = [zhead("v_a", kh) for kh in kvs]
    kk = [jnp.concatenate([kprev_ref[kh], kb[kh]], axis=0) for kh in kvs]
    vv = [jnp.concatenate([vprev_ref[kh], vb[kh]], axis=0) for kh in kvs]
    qb = [rot_a(_rms_rows(zhead("q_a", qh), qn_ref[...])).astype(BF16) for qh in all_heads]
    s_all = [_dot_nt(jnp.concatenate(qb[kh * GQA_GROUP:(kh + 1) * GQA_GROUP], axis=0), kk[kh])
             * (HEAD_DIM ** -0.5) for kh in kvs]
    sinks = [sink_ref[qh] for qh in all_heads]
    s = [jnp.where(mask, s_all[qh // GQA_GROUP][(qh % GQA_GROUP) * WINDOW:
                                                (qh % GQA_GROUP + 1) * WINDOW, :], neg)
         for qh in all_heads]
    mx = [jnp.maximum(jnp.max(s[qh], axis=-1, keepdims=True), sinks[qh]) for qh in all_heads]
    p = [jnp.exp(s[qh] - mx[qh]) for qh in all_heads]
    inv_denoms = [1.0 / (jnp.sum(p[qh], axis=-1, keepdims=True) + jnp.exp(sinks[qh] - mx[qh]))
                  for qh in all_heads]
    probs = [p[qh].astype(BF16) for qh in all_heads]
    o_all = [jnp.dot(jnp.concatenate(probs[kh * GQA_GROUP:(kh + 1) * GQA_GROUP], axis=0), vv[kh],
                     preferred_element_type=F32) for kh in kvs]
    for qh in all_heads:
        gq = qh % GQA_GROUP
        o = o_all[qh // GQA_GROUP][gq * WINDOW:(gq + 1) * WINDOW, :] * inv_denoms[qh]
        col = RET_WIDTH + qh * HEAD_DIM
        y_ref[rows, col:col + HEAD_DIM] = o.astype(BF16)
    for kh in kvs:
        kprev_ref[kh] = kb[kh]
        vprev_ref[kh] = vb[kh]
    yield


def _mix_layer_kernel(sink_ref, gam_ref, xn_ref, gain_ref, win_ref, cb_r_ref, sb_r_ref,
                      cb_a_ref, sb_a_ref, off_ref, lane_ref, qn_ref, kn_ref, wout_ref, o_ref,
                      zfa_ref, zva_ref, zfb_ref, zvb_ref, h_ref, xprev_ref, y_ref, state_ref,
                      kprev_ref, vprev_ref, *, proj_tile):
    s = pl.program_id(0)
    last = pl.num_programs(0) - 1
    rb = xn_ref.shape[0]
    base_refs = (cb_r_ref, sb_r_ref, cb_a_ref, sb_a_ref)
    z_a = {False: zfa_ref, True: zva_ref}
    z_b = {False: zfb_ref, True: zvb_ref}

    def step(project_into, mix_from):
        pieces = iter(())
        if mix_from is not None:
            off = off_ref[0]
            lane = lane_ref[...]

            def chunk_pieces(cc):
                rows = slice(cc * CHUNK, (cc + 1) * CHUNK)

                def zhead(name, hd):
                    is_value, z_off, _ = _Z_LAYOUT[name]
                    return mix_from[is_value][rows,
                                              z_off + hd * HEAD_DIM: z_off + (hd + 1) * HEAD_DIM]

                not_first = (s > 1) if cc == 0 else True
                return _mix_chunk(zhead, rows, not_first, sink_ref, gam_ref, base_refs, off,
                                  lane, qn_ref, kn_ref, y_ref, state_ref, kprev_ref, vprev_ref)

            pieces = itertools.chain(*[chunk_pieces(cc) for cc in range(rb // CHUNK)])

        if project_into is not None:
            n_tiles = IN_WIDTH // proj_tile
            n_pieces = (rb // CHUNK) * (RET_HEADS // RET_GROUP + 1)
            h_ref[...] = _rms_rows(xn_ref[...], gain_ref[...]).astype(BF16)
            for t in range(n_tiles):
                w_col = t * proj_tile
                is_value, z_col = _z_dest(w_col)
                dst = project_into[is_value]
                tile = jnp.dot(h_ref[...], win_ref[:, w_col:w_col + proj_tile],
                               preferred_element_type=F32)
                dst[:, z_col:z_col + proj_tile] = tile.astype(dst.dtype)
                if (t + 1) * n_pieces // n_tiles > t * n_pieces // n_tiles:
                    next(pieces, None)
        for _ in pieces:
            pass

        if mix_from is not None:
            o_ref[...] = xprev_ref[...] + jnp.dot(y_ref[...], wout_ref[...],
                                                  preferred_element_type=F32)
        if project_into is not None:
            xprev_ref[...] = xn_ref[...]

    @pl.when(s == 0)
    def _():
        state_ref[...] = jnp.zeros_like(state_ref)
        kprev_ref[...] = jnp.zeros_like(kprev_ref)
        vprev_ref[...] = jnp.zeros_like(vprev_ref)
        step(z_b, None)

    @pl.when((s > 0) & (s < last))
    def _():
        step(z_a, z_b)
        for is_value in (False, True):
            z_b[is_value][...] = z_a[is_value][...]

    @pl.when(s == last)
    def _():
        step(None, z_b)


def _mix_layer(x, gain, w_in, w_out, sinks, q_gain, k_gain, *, rb=256, proj_tile=256):
    s, d = x.shape
    nb = s // rb
    t = _position_tables(rb, nb)
    cur = lambda i: (jnp.minimum(i, nb - 1), 0)
    prv = lambda i: (jnp.maximum(i - 1, 0), 0)
    const2 = lambda i: (0, 0)
    resident = dict(pipeline_mode=pl.Buffered(1))
    base_tab = pl.BlockSpec((rb, HEAD_DIM), const2, **resident)
    smem = pl.BlockSpec(memory_space=pltpu.SMEM)
    return pl.pallas_call(
        functools.partial(_mix_layer_kernel, proj_tile=proj_tile),
        name="mix_layer",
        grid=(nb + 1,),
        in_specs=[
            smem, smem,
            pl.BlockSpec((rb, d), cur),
            pl.BlockSpec((1, d), const2),
            pl.BlockSpec(w_in.shape, const2, **resident),
            base_tab, base_tab, base_tab, base_tab,
            pl.BlockSpec((1,) + t["block_off"].shape[1:], lambda i: (jnp.maximum(i - 1, 0), 0, 0)),
            pl.BlockSpec(t["lane"].shape, const2),
            pl.BlockSpec((1, HEAD_DIM), const2), pl.BlockSpec((1, HEAD_DIM), const2),
            pl.BlockSpec(w_out.shape, const2, **resident),
        ],
        out_specs=pl.BlockSpec((rb, d), prv),
        out_shape=jax.ShapeDtypeStruct((s, d), F32),
        scratch_shapes=[
            pltpu.VMEM((rb, ZF_WIDTH), F32), pltpu.VMEM((rb, ZV_WIDTH), BF16),
            pltpu.VMEM((rb, ZF_WIDTH), F32), pltpu.VMEM((rb, ZV_WIDTH), BF16),
            pltpu.VMEM((rb, d), BF16),
            pltpu.VMEM((rb, d), F32),
            pltpu.VMEM((rb, MIX_WIDTH), BF16),
            pltpu.VMEM((RET_HEADS, HEAD_DIM, HEAD_DIM), F32),
            pltpu.VMEM((ATT_KV_HEADS, WINDOW, HEAD_DIM), BF16),
            pltpu.VMEM((ATT_KV_HEADS, WINDOW, HEAD_DIM), BF16),
        ],
        compiler_params=pltpu.CompilerParams(
            dimension_semantics=("arbitrary",),
            vmem_limit_bytes=VMEM_LIMIT_BYTES,
        ),
    )(sinks, t["gamma"], x, gain, w_in, t["cos_base_r"], t["sin_base_r"], t["cos_base_a"],
      t["sin_base_a"], t["block_off"], t["lane"], q_gain, k_gain, w_out)


def _position_tables(rb, nb):
    f32 = lambda a: jnp.asarray(np.asarray(a, np.float32))
    r = np.arange(rb, dtype=np.float64)[:, None]
    start = (np.arange(nb, dtype=np.float64) * rb)[:, None]

    inv_r = RET_THETA ** (-np.arange(0, HEAD_DIM, 2, dtype=np.float64) / HEAD_DIM)
    inv_r = np.concatenate([inv_r, inv_r])
    inv_a = ROPE_THETA ** (-np.arange(0, ROPE_DIM, 2, dtype=np.float64) / ROPE_DIM)
    inv_a = np.concatenate([inv_a, inv_a, np.zeros(HEAD_DIM - ROPE_DIM)])

    block_off = np.stack([np.cos(start * inv_r), np.sin(start * inv_r),
                          np.cos(start * inv_a), np.sin(start * inv_a)], axis=1)

    half = ROPE_DIM // 2
    lane = np.zeros((8, HEAD_DIM))
    lane[LANE_SIGN_R] = np.where(np.arange(HEAD_DIM) < HEAD_DIM // 2, -1.0, 1.0)
    lane[LANE_MASK_LO, :half] = -1.0
    lane[LANE_MASK_HI, half:ROPE_DIM] = 1.0

    log_g = np.log1p(-np.exp2(-5.0 - np.arange(RET_HEADS, dtype=np.float64)))
    gamma = np.stack([log_g, np.exp(CHUNK * log_g)])
    return {
        "cos_base_r": f32(np.cos(r * inv_r)), "sin_base_r": f32(np.sin(r * inv_r)),
        "cos_base_a": f32(np.cos(r * inv_a)), "sin_base_a": f32(np.sin(r * inv_a)),
        "block_off": f32(block_off), "lane": f32(lane), "gamma": f32(gamma),
    }


def kernel(x, ffn1_norm, ffn1_w_gate, ffn1_w_up, ffn1_w_down, mix_norm, w_in, q_norm, k_norm,
           attn_sinks, w_out, ffn2_norm, ffn2_w_gate, ffn2_w_up, ffn2_w_down):
    b, s, d = x.shape
    depth = ffn1_norm.shape[0]
    outs = []
    for bi in range(b):
        xb = x[bi]
        for l in range(depth):
            xb, w_in_bf16, w_out_bf16 = _ffn(
                xb, ffn1_norm[l][None], ffn1_w_gate[l], ffn1_w_up[l], ffn1_w_down[l],
                cast=(w_in[l], w_out[l]))
            xb = _mix_layer(xb, mix_norm[l][None], w_in_bf16, w_out_bf16, attn_sinks[l],
                            q_norm[l][None], k_norm[l][None])
            xb, = _ffn(xb, ffn2_norm[l][None], ffn2_w_gate[l], ffn2_w_up[l], ffn2_w_down[l])
        outs.append(xb)
    return jnp.stack(outs, axis=0)
```

```python
import functools
import itertools

import jax
import jax.numpy as jnp
import numpy as np
from jax import lax
from jax.experimental import pallas as pl
from jax.experimental.pallas import tpu as pltpu

D_MODEL = 2048
HEAD_DIM = 128
RET_HEADS = 8
ATT_Q_HEADS = 8
ATT_KV_HEADS = 2
GQA_GROUP = ATT_Q_HEADS // ATT_KV_HEADS
RET_WIDTH = RET_HEADS * HEAD_DIM
ATT_WIDTH = ATT_Q_HEADS * HEAD_DIM
KV_WIDTH = ATT_KV_HEADS * HEAD_DIM
MIX_WIDTH = RET_WIDTH + ATT_WIDTH
IN_WIDTH = 4 * RET_WIDTH + ATT_WIDTH + 2 * KV_WIDTH
CHUNK = 128
WINDOW = 128
ROPE_THETA = 500000.0
ROPE_DIM = HEAD_DIM // 4
RET_THETA = 10000.0
EPS = 1e-6

VMEM_LIMIT_BYTES = 62 * 1024 * 1024

F32 = jnp.float32
BF16 = jnp.bfloat16


def _rms_rows(x, gain):
    ms = jnp.mean(x * x, axis=-1, keepdims=True)
    return x * lax.rsqrt(ms + EPS) * gain


FFN_FIRST_FETCH_STEP = 2


def _ffn_kernel(*refs, row_chunk, n_cast):
    x_hbm, gain_ref, wg_ref, wu_ref, wd_ref = refs[:5]
    cast_src = refs[5:5 + n_cast]
    o_ref = refs[5 + n_cast]
    cast_dst = refs[6 + n_cast:6 + 2 * n_cast]
    h_ref, xbuf_ref, sem = refs[6 + 2 * n_cast:]
    i = pl.program_id(0)
    j = pl.program_id(1)
    n_tiles = pl.num_programs(0)
    tm = o_ref.shape[0]

    n_chunks = tm // row_chunk
    first_fetch_step = FFN_FIRST_FETCH_STEP

    def x_copy(tile, r):
        r0 = r * row_chunk
        if not isinstance(r0, int):
            r0 = pl.multiple_of(r0, row_chunk)
        return pltpu.make_async_copy(x_hbm.at[pl.ds(tile * tm + r0, row_chunk), :],
                                     xbuf_ref.at[pl.ds(r0, row_chunk), :], sem)

    @pl.when((i == 0) & (j == 0))
    def _():
        for r in range(n_chunks):
            x_copy(0, r).start()

    @pl.when(j == 0)
    def _():
        for r in range(n_chunks):
            x_copy(i, r).wait()
        for r in range(n_chunks):
            rows = pl.ds(r * row_chunk, row_chunk)
            x = xbuf_ref[rows, :]
            h_ref[rows, :] = _rms_rows(x, gain_ref[...]).astype(BF16)
            o_ref[rows, :] = x

    @pl.when((j >= first_fetch_step) & (j < first_fetch_step + n_chunks) & (i + 1 < n_tiles))
    def _():
        x_copy(i + 1, j - first_fetch_step).start(priority=1)

    for src, dst in zip(cast_src, cast_dst):
        dst[...] = src[...].astype(BF16)

    h = h_ref[...]
    g = jnp.dot(h, wg_ref[...].astype(BF16), preferred_element_type=F32)
    u = jnp.dot(h, wu_ref[...].astype(BF16), preferred_element_type=F32)
    a = (0.5 * (g * jax.nn.sigmoid(g)) * u).astype(BF16)
    o_ref[...] += jnp.dot(a, wd_ref[...].astype(BF16), preferred_element_type=F32)


def _ffn(x, gain, wg, wu, wd, cast=(), *, tm=1024, tf=512, row_chunk=128):
    s, d = x.shape
    d_ff = wg.shape[1]
    n_i, n_j = s // tm, d_ff // tf
    assert FFN_FIRST_FETCH_STEP + tm // row_chunk <= n_j
    cast_specs = []
    for w in cast:
        n_blocks = max(n for n in range(1, n_i * n_j + 1)
                       if w.shape[0] % n == 0 and (w.shape[0] // n) % 16 == 0)
        index = lambda i, j, n_blocks=n_blocks: (jnp.minimum(i * n_j + j, n_blocks - 1), 0)
        cast_specs.append(pl.BlockSpec((w.shape[0] // n_blocks, w.shape[1]), index))
    outs = pl.pallas_call(
        functools.partial(_ffn_kernel, row_chunk=row_chunk, n_cast=len(cast)),
        name="ffn",
        grid=(n_i, n_j),
        in_specs=[
            pl.BlockSpec(memory_space=pl.ANY),
            pl.BlockSpec((1, d), lambda i, j: (0, 0)),
            pl.BlockSpec((d, tf), lambda i, j: (0, j)),
            pl.BlockSpec((d, tf), lambda i, j: (0, j)),
            pl.BlockSpec((tf, d), lambda i, j: (j, 0)),
            *cast_specs,
        ],
        out_specs=[pl.BlockSpec((tm, d), lambda i, j: (i, 0)), *cast_specs],
        out_shape=[jax.ShapeDtypeStruct((s, d), F32),
                   *[jax.ShapeDtypeStruct(w.shape, BF16) for w in cast]],
        scratch_shapes=[pltpu.VMEM((tm, d), BF16), pltpu.VMEM((tm, d), F32),
                        pltpu.SemaphoreType.DMA(())],
        compiler_params=pltpu.CompilerParams(
            dimension_semantics=("arbitrary", "arbitrary"),
            vmem_limit_bytes=VMEM_LIMIT_BYTES,
        ),
    )(x, gain, wg, wu, wd, *cast)
    return tuple(outs)


_SEGMENTS = (("q_r", RET_WIDTH, False), ("k_r", RET_WIDTH, False), ("v_r", RET_WIDTH, True),
             ("g_r", RET_WIDTH, False), ("q_a", ATT_WIDTH, False), ("k_a", KV_WIDTH, False),
             ("v_a", KV_WIDTH, True))


def _z_layout():
    layout, w_off, widths = {}, 0, {False: 0, True: 0}
    for name, width, is_value in _SEGMENTS:
        layout[name] = (is_value, widths[is_value], w_off)
        widths[is_value] += width
        w_off += width
    return layout, widths[False], widths[True]


_Z_LAYOUT, ZF_WIDTH, ZV_WIDTH = _z_layout()


def _z_dest(w_col):
    for name, width, _ in _SEGMENTS:
        is_value, z_off, w_off = _Z_LAYOUT[name]
        if w_off <= w_col < w_off + width:
            return is_value, z_off + (w_col - w_off)
    raise ValueError(w_col)


def _dot_nt(a, b):
    return lax.dot_general(a, b, (((1,), (1,)), ((), ())), preferred_element_type=F32)


def _dot_tn(a, b):
    return lax.dot_general(a, b, (((0,), (0,)), ((), ())), preferred_element_type=F32)


LANE_SIGN_R, LANE_MASK_LO, LANE_MASK_HI = 0, 1, 2
OFF_COS_R, OFF_SIN_R, OFF_COS_A, OFF_SIN_A = 0, 1, 2, 3
GAM_LOG, GAM_CHUNK = 0, 1
RET_GROUP = 8


def _mix_chunk(zhead, rows, not_first, sink_ref, gam_ref, base_refs, off, lane, qn_ref, kn_ref,
               y_ref, state_ref, kprev_ref, vprev_ref):
    cb_r_ref, sb_r_ref, cb_a_ref, sb_a_ref = base_refs
    row = lambda t, k: t[k:k + 1, :]

    cb, sb = cb_r_ref[rows, :], sb_r_ref[rows, :]
    co, so = row(off, OFF_COS_R), row(off, OFF_SIN_R)
    cos_r = cb * co - sb * so
    sin_r = (sb * co + cb * so) * row(lane, LANE_SIGN_R)

    def rot_r(t):
        return t * cos_r + pltpu.roll(t, HEAD_DIM // 2, 1) * sin_r

    n = lax.broadcasted_iota(jnp.int32, (CHUNK, CHUNK), 0)
    m = lax.broadcasted_iota(jnp.int32, (CHUNK, CHUNK), 1)
    causal = n >= m
    lag = jnp.maximum(n - m, 0).astype(F32)
    n_plus_1 = (n + 1).astype(F32)
    to_end = (CHUNK - 1 - n).astype(F32)

    for h0 in range(0, RET_HEADS, RET_GROUP):
        hs = range(h0, h0 + RET_GROUP)
        log_g = [gam_ref[GAM_LOG, h] for h in hs]
        q = [rot_r(zhead("q_r", h)) for h in hs]
        k = [rot_r(zhead("k_r", h)) * (HEAD_DIM ** -0.5) for h in hs]
        vb = [zhead("v_r", h) for h in hs]
        scores = [_dot_nt(q[i].astype(BF16), k[i].astype(BF16))
                  * jnp.where(causal, jnp.exp(lag * log_g[i]), 0.0) for i in range(RET_GROUP)]
        prev = [state_ref[h] for h in hs]
        y = [jnp.dot(jnp.concatenate([scores[i].astype(BF16),
                                      (q[i] * jnp.exp(n_plus_1 * log_g[i])).astype(BF16)], axis=1),
                     jnp.concatenate([vb[i], prev[i].astype(BF16)], axis=0),
                     preferred_element_type=F32) for i in range(RET_GROUP)]
        chunk_kv = [_dot_tn((k[i] * jnp.exp(to_end * log_g[i])).astype(BF16), vb[i])
                    for i in range(RET_GROUP)]
        for i, h in enumerate(hs):
            state_ref[h] = prev[i] * gam_ref[GAM_CHUNK, h] + chunk_kv[i]
        y = [y[i] * lax.rsqrt(jnp.mean(y[i] * y[i], axis=-1, keepdims=True) + EPS)
             for i in range(RET_GROUP)]
        for i, h in enumerate(hs):
            g = zhead("g_r", h)
            y_ref[rows, h * HEAD_DIM:(h + 1) * HEAD_DIM] = (
                (g * jax.nn.sigmoid(g)) * y[i]).astype(BF16)
        yield

    cb, sb = cb_a_ref[rows, :], sb_a_ref[rows, :]
    co, so = row(off, OFF_COS_A), row(off, OFF_SIN_A)
    cos_a = cb * co - sb * so
    sin_a = sb * co + cb * so
    sin_lo = sin_a * row(lane, LANE_MASK_LO)
    sin_hi = sin_a * row(lane, LANE_MASK_HI)
    half = ROPE_DIM // 2

    def rot_a(t):
        return (t * cos_a + pltpu.roll(t, HEAD_DIM - half, 1) * sin_lo
                + pltpu.roll(t, half, 1) * sin_hi)

    qi = lax.broadcasted_iota(jnp.int32, (WINDOW, 2 * WINDOW), 0)
    kj = lax.broadcasted_iota(jnp.int32, (WINDOW, 2 * WINDOW), 1)
    rel = WINDOW + qi - kj
    mask = (rel >= 0) & (rel < WINDOW) & ((kj >= WINDOW) | not_first)
    neg = jnp.finfo(F32).min

    kvs = range(ATT_KV_HEADS)
    all_heads = range(ATT_Q_HEADS)
    kb = [rot_a(_rms_rows(zhead("k_a", kh), kn_ref[...])).astype(BF16) for kh in kvs]
    vb = [zhead("v_a", kh) for kh in kvs]
    kk = [jnp.concatenate([kprev_ref[kh], kb[kh]], axis=0) for kh in kvs]
    vv = [jnp.concatenate([vprev_ref[kh], vb[kh]], axis=0) for kh in kvs]
    qb = [rot_a(_rms_rows(zhead("q_a", qh), qn_ref[...])).astype(BF16) for qh in all_heads]
    s_all = [_dot_nt(jnp.concatenate(qb[kh * GQA_GROUP:(kh + 1) * GQA_GROUP], axis=0), kk[kh])
             * (HEAD_DIM ** -0.5) for kh in kvs]
    sinks = [sink_ref[qh] for qh in all_heads]
    s = [jnp.where(mask, s_all[qh // GQA_GROUP][(qh % GQA_GROUP) * WINDOW:
                                                (qh % GQA_GROUP + 1) * WINDOW, :], neg)
         for qh in all_heads]
    mx = [jnp.maximum(jnp.max(s[qh], axis=-1, keepdims=True), sinks[qh]) for qh in all_heads]
    p = [jnp.exp(s[qh] - mx[qh]) for qh in all_heads]
    inv_denoms = [1.0 / (jnp.sum(p[qh], axis=-1, keepdims=True) + jnp.exp(sinks[qh] - mx[qh]))
                  for qh in all_heads]
    probs = [p[qh].astype(BF16) for qh in all_heads]
    o_all = [jnp.dot(jnp.concatenate(probs[kh * GQA_GROUP:(kh + 1) * GQA_GROUP], axis=0), vv[kh],
                     preferred_element_type=F32) for kh in kvs]
    for qh in all_heads:
        gq = qh % GQA_GROUP
        o = o_all[qh // GQA_GROUP][gq * WINDOW:(gq + 1) * WINDOW, :] * inv_denoms[qh]
        col = RET_WIDTH + qh * HEAD_DIM
        y_ref[rows, col:col + HEAD_DIM] = o.astype(BF16)
    for kh in kvs:
        kprev_ref[kh] = kb[kh]
        vprev_ref[kh] = vb[kh]
    yield


def _mix_layer_kernel(sink_ref, gam_ref, xn_ref, gain_ref, win_ref, cb_r_ref, sb_r_ref,
                      cb_a_ref, sb_a_ref, off_ref, lane_ref, qn_ref, kn_ref, wout_ref, o_ref,
                      zfa_ref, zva_ref, zfb_ref, zvb_ref, h_ref, xprev_ref, y_ref, state_ref,
                      kprev_ref, vprev_ref, *, proj_tile):
    s = pl.program_id(0)
    last = pl.num_programs(0) - 1
    rb = xn_ref.shape[0]
    base_refs = (cb_r_ref, sb_r_ref, cb_a_ref, sb_a_ref)
    z_a = {False: zfa_ref, True: zva_ref}
    z_b = {False: zfb_ref, True: zvb_ref}

    def step(project_into, mix_from):
        pieces = iter(())
        if mix_from is not None:
            off = off_ref[0]
            lane = lane_ref[...]

            def chunk_pieces(cc):
                rows = slice(cc * CHUNK, (cc + 1) * CHUNK)

                def zhead(name, hd):
                    is_value, z_off, _ = _Z_LAYOUT[name]
                    return mix_from[is_value][rows,
                                              z_off + hd * HEAD_DIM: z_off + (hd + 1) * HEAD_DIM]

                not_first = (s > 1) if cc == 0 else True
                return _mix_chunk(zhead, rows, not_first, sink_ref, gam_ref, base_refs, off,
                                  lane, qn_ref, kn_ref, y_ref, state_ref, kprev_ref, vprev_ref)

            pieces = itertools.chain(*[chunk_pieces(cc) for cc in range(rb // CHUNK)])

        if project_into is not None:
            n_tiles = IN_WIDTH // proj_tile
            n_pieces = (rb // CHUNK) * (RET_HEADS // RET_GROUP + 1)
            h_ref[...] = _rms_rows(xn_ref[...], gain_ref[...]).astype(BF16)
            for t in range(n_tiles):
                w_col = t * proj_tile
                is_value, z_col = _z_dest(w_col)
                dst = project_into[is_value]
                tile = jnp.dot(h_ref[...], win_ref[:, w_col:w_col + proj_tile],
                               preferred_element_type=F32)
                dst[:, z_col:z_col + proj_tile] = tile.astype(dst.dtype)
                if (t + 1) * n_pieces // n_tiles > t * n_pieces // n_tiles:
                    next(pieces, None)
        for _ in pieces:
            pass

        if mix_from is not None:
            o_ref[...] = xprev_ref[...] + jnp.dot(y_ref[...], wout_ref[...],
                                                  preferred_element_type=F32)
        if project_into is not None:
            xprev_ref[...] = xn_ref[...]

    @pl.when(s == 0)
    def _():
        state_ref[...] = jnp.zeros_like(state_ref)
        kprev_ref[...] = jnp.zeros_like(kprev_ref)
        vprev_ref[...] = jnp.zeros_like(vprev_ref)
        step(z_b, None)

    @pl.when((s > 0) & (s < last))
    def _():
        step(z_a, z_b)
        for is_value in (False, True):
            z_b[is_value][...] = z_a[is_value][...]

    @pl.when(s == last)
    def _():
        step(None, z_b)


def _mix_layer(x, gain, w_in, w_out, sinks, q_gain, k_gain, *, rb=256, proj_tile=256):
    s, d = x.shape
    nb = s // rb
    t = _position_tables(rb, nb)
    cur = lambda i: (jnp.minimum(i, nb - 1), 0)
    prv = lambda i: (jnp.maximum(i - 1, 0), 0)
    const2 = lambda i: (0, 0)
    resident = dict(pipeline_mode=pl.Buffered(1))
    base_tab = pl.BlockSpec((rb, HEAD_DIM), const2, **resident)
    smem = pl.BlockSpec(memory_space=pltpu.SMEM)
    return pl.pallas_call(
        functools.partial(_mix_layer_kernel, proj_tile=proj_tile),
        name="mix_layer",
        grid=(nb + 1,),
        in_specs=[
            smem, smem,
            pl.BlockSpec((rb, d), cur),
            pl.BlockSpec((1, d), const2),
            pl.BlockSpec(w_in.shape, const2, **resident),
            base_tab, base_tab, base_tab, base_tab,
            pl.BlockSpec((1,) + t["block_off"].shape[1:], lambda i: (jnp.maximum(i - 1, 0), 0, 0)),
            pl.BlockSpec(t["lane"].shape, const2),
            pl.BlockSpec((1, HEAD_DIM), const2), pl.BlockSpec((1, HEAD_DIM), const2),
            pl.BlockSpec(w_out.shape, const2, **resident),
        ],
        out_specs=pl.BlockSpec((rb, d), prv),
        out_shape=jax.ShapeDtypeStruct((s, d), F32),
        scratch_shapes=[
            pltpu.VMEM((rb, ZF_WIDTH), F32), pltpu.VMEM((rb, ZV_WIDTH), BF16),
            pltpu.VMEM((rb, ZF_WIDTH), F32), pltpu.VMEM((rb, ZV_WIDTH), BF16),
            pltpu.VMEM((rb, d), BF16),
            pltpu.VMEM((rb, d), F32),
            pltpu.VMEM((rb, MIX_WIDTH), BF16),
            pltpu.VMEM((RET_HEADS, HEAD_DIM, HEAD_DIM), F32),
            pltpu.VMEM((ATT_KV_HEADS, WINDOW, HEAD_DIM), BF16),
            pltpu.VMEM((ATT_KV_HEADS, WINDOW, HEAD_DIM), BF16),
        ],
        compiler_params=pltpu.CompilerParams(
            dimension_semantics=("arbitrary",),
            vmem_limit_bytes=VMEM_LIMIT_BYTES,
        ),
    )(sinks, t["gamma"], x, gain, w_in, t["cos_base_r"], t["sin_base_r"], t["cos_base_a"],
      t["sin_base_a"], t["block_off"], t["lane"], q_gain, k_gain, w_out)


def _position_tables(rb, nb):
    f32 = lambda a: jnp.asarray(np.asarray(a, np.float32))
    r = np.arange(rb, dtype=np.float64)[:, None]
    start = (np.arange(nb, dtype=np.float64) * rb)[:, None]

    inv_r = RET_THETA ** (-np.arange(0, HEAD_DIM, 2, dtype=np.float64) / HEAD_DIM)
    inv_r = np.concatenate([inv_r, inv_r])
    inv_a = ROPE_THETA ** (-np.arange(0, ROPE_DIM, 2, dtype=np.float64) / ROPE_DIM)
    inv_a = np.concatenate([inv_a, inv_a, np.zeros(HEAD_DIM - ROPE_DIM)])

    block_off = np.stack([np.cos(start * inv_r), np.sin(start * inv_r),
                          np.cos(start * inv_a), np.sin(start * inv_a)], axis=1)

    half = ROPE_DIM // 2
    lane = np.zeros((8, HEAD_DIM))
    lane[LANE_SIGN_R] = np.where(np.arange(HEAD_DIM) < HEAD_DIM // 2, -1.0, 1.0)
    lane[LANE_MASK_LO, :half] = -1.0
    lane[LANE_MASK_HI, half:ROPE_DIM] = 1.0

    log_g = np.log1p(-np.exp2(-5.0 - np.arange(RET_HEADS, dtype=np.float64)))
    gamma = np.stack([log_g, np.exp(CHUNK * log_g)])
    return {
        "cos_base_r": f32(np.cos(r * inv_r)), "sin_base_r": f32(np.sin(r * inv_r)),
        "cos_base_a": f32(np.cos(r * inv_a)), "sin_base_a": f32(np.sin(r * inv_a)),
        "block_off": f32(block_off), "lane": f32(lane), "gamma": f32(gamma),
    }


def kernel(x, ffn1_norm, ffn1_w_gate, ffn1_w_up, ffn1_w_down, mix_norm, w_in, q_norm, k_norm,
           attn_sinks, w_out, ffn2_norm, ffn2_w_gate, ffn2_w_up, ffn2_w_down):
    b, s, d = x.shape
    depth = ffn1_norm.shape[0]
    outs = []
    for bi in range(b):
        xb = x[bi]
        for l in range(depth):
            xb, w_in_bf16, w_out_bf16 = _ffn(
                xb, ffn1_norm[l][None], ffn1_w_gate[l], ffn1_w_up[l], ffn1_w_down[l],
                cast=(w_in[l], w_out[l]))
            xb = _mix_layer(xb, mix_norm[l][None], w_in_bf16, w_out_bf16, attn_sinks[l],
                            q_norm[l][None], k_norm[l][None])
            xb, = _ffn(xb, ffn2_norm[l][None], ffn2_w_gate[l], ffn2_w_up[l], ffn2_w_down[l])
        outs.append(xb)
    return jnp.stack(outs, axis=0)
```

```python
import functools
import itertools

import jax
import jax.numpy as jnp
import numpy as np
from jax import lax
from jax.experimental import pallas as pl
from jax.experimental.pallas import tpu as pltpu

D_MODEL = 2048
HEAD_DIM = 128
RET_HEADS = 8
ATT_Q_HEADS = 8
ATT_KV_HEADS = 2
GQA_GROUP = ATT_Q_HEADS // ATT_KV_HEADS
RET_WIDTH = RET_HEADS * HEAD_DIM
ATT_WIDTH = ATT_Q_HEADS * HEAD_DIM
KV_WIDTH = ATT_KV_HEADS * HEAD_DIM
MIX_WIDTH = RET_WIDTH + ATT_WIDTH
IN_WIDTH = 4 * RET_WIDTH + ATT_WIDTH + 2 * KV_WIDTH
CHUNK = 128
WINDOW = 128
ROPE_THETA = 500000.0
ROPE_DIM = HEAD_DIM // 4
RET_THETA = 10000.0
EPS = 1e-6

VMEM_LIMIT_BYTES = 62 * 1024 * 1024

F32 = jnp.float32
BF16 = jnp.bfloat16


def _rms_rows(x, gain):
    ms = jnp.mean(x * x, axis=-1, keepdims=True)
    return x * lax.rsqrt(ms + EPS) * gain


FFN_FIRST_FETCH_STEP = 2


def _ffn_kernel(*refs, row_chunk, n_cast):
    x_hbm, gain_ref, wg_ref, wu_ref, wd_ref = refs[:5]
    cast_src = refs[5:5 + n_cast]
    o_ref = refs[5 + n_cast]
    cast_dst = refs[6 + n_cast:6 + 2 * n_cast]
    h_ref, xbuf_ref, sem = refs[6 + 2 * n_cast:]
    i = pl.program_id(0)
    j = pl.program_id(1)
    n_tiles = pl.num_programs(0)
    tm = o_ref.shape[0]

    n_chunks = tm // row_chunk
    first_fetch_step = FFN_FIRST_FETCH_STEP

    def x_copy(tile, r):
        r0 = r * row_chunk
        if not isinstance(r0, int):
            r0 = pl.multiple_of(r0, row_chunk)
        return pltpu.make_async_copy(x_hbm.at[pl.ds(tile * tm + r0, row_chunk), :],
                                     xbuf_ref.at[pl.ds(r0, row_chunk), :], sem)

    @pl.when((i == 0) & (j == 0))
    def _():
        for r in range(n_chunks):
            x_copy(0, r).start()

    @pl.when(j == 0)
    def _():
        for r in range(n_chunks):
            x_copy(i, r).wait()
        for r in range(n_chunks):
            rows = pl.ds(r * row_chunk, row_chunk)
            x = xbuf_ref[rows, :]
            h_ref[rows, :] = _rms_rows(x, gain_ref[...]).astype(BF16)
            o_ref[rows, :] = x

    @pl.when((j >= first_fetch_step) & (j < first_fetch_step + n_chunks) & (i + 1 < n_tiles))
    def _():
        x_copy(i + 1, j - first_fetch_step).start()

    for src, dst in zip(cast_src, cast_dst):
        dst[...] = src[...].astype(BF16)

    h = h_ref[...]
    g = jnp.dot(h, wg_ref[...].astype(BF16), preferred_element_type=F32)
    u = jnp.dot(h, wu_ref[...].astype(BF16), preferred_element_type=F32)
    a = (0.5 * (g * jax.nn.sigmoid(g)) * u).astype(BF16)
    o_ref[...] += jnp.dot(a, wd_ref[...].astype(BF16), preferred_element_type=F32)


def _ffn(x, gain, wg, wu, wd, cast=(), *, tm=1024, tf=512, row_chunk=128):
    s, d = x.shape
    d_ff = wg.shape[1]
    n_i, n_j = s // tm, d_ff // tf
    assert FFN_FIRST_FETCH_STEP + tm // row_chunk <= n_j
    cast_specs = []
    for w in cast:
        n_blocks = max(n for n in range(1, n_i * n_j + 1)
                       if w.shape[0] % n == 0 and (w.shape[0] // n) % 16 == 0)
        index = lambda i, j, n_blocks=n_blocks: (jnp.minimum(i * n_j + j, n_blocks - 1), 0)
        cast_specs.append(pl.BlockSpec((w.shape[0] // n_blocks, w.shape[1]), index))
    outs = pl.pallas_call(
        functools.partial(_ffn_kernel, row_chunk=row_chunk, n_cast=len(cast)),
        name="ffn",
        grid=(n_i, n_j),
        in_specs=[
            pl.BlockSpec(memory_space=pl.ANY),
            pl.BlockSpec((1, d), lambda i, j: (0, 0)),
            pl.BlockSpec((d, tf), lambda i, j: (0, j)),
            pl.BlockSpec((d, tf), lambda i, j: (0, j)),
            pl.BlockSpec((tf, d), lambda i, j: (j, 0)),
            *cast_specs,
        ],
        out_specs=[pl.BlockSpec((tm, d), lambda i, j: (i, 0)), *cast_specs],
        out_shape=[jax.ShapeDtypeStruct((s, d), F32),
                   *[jax.ShapeDtypeStruct(w.shape, BF16) for w in cast]],
        scratch_shapes=[pltpu.VMEM((tm, d), BF16), pltpu.VMEM((tm, d), F32),
                        pltpu.SemaphoreType.DMA(())],
        compiler_params=pltpu.CompilerParams(
            dimension_semantics=("arbitrary", "arbitrary"),
            vmem_limit_bytes=VMEM_LIMIT_BYTES,
        ),
    )(x, gain, wg, wu, wd, *cast)
    return tuple(outs)


_SEGMENTS = (("q_r", RET_WIDTH, False), ("k_r", RET_WIDTH, False), ("v_r", RET_WIDTH, True),
             ("g_r", RET_WIDTH, False), ("q_a", ATT_WIDTH, False), ("k_a", KV_WIDTH, False),
             ("v_a", KV_WIDTH, True))


def _z_layout():
    layout, w_off, widths = {}, 0, {False: 0, True: 0}
    for name, width, is_value in _SEGMENTS:
        layout[name] = (is_value, widths[is_value], w_off)
        widths[is_value] += width
        w_off += width
    return layout, widths[False], widths[True]


_Z_LAYOUT, ZF_WIDTH, ZV_WIDTH = _z_layout()


def _z_dest(w_col):
    for name, width, _ in _SEGMENTS:
        is_value, z_off, w_off = _Z_LAYOUT[name]
        if w_off <= w_col < w_off + width:
            return is_value, z_off + (w_col - w_off)
    raise ValueError(w_col)


def _dot_nt(a, b):
    return lax.dot_general(a, b, (((1,), (1,)), ((), ())), preferred_element_type=F32)


def _dot_tn(a, b):
    return lax.dot_general(a, b, (((0,), (0,)), ((), ())), preferred_element_type=F32)


LANE_SIGN_R, LANE_MASK_LO, LANE_MASK_HI = 0, 1, 2
OFF_COS_R, OFF_SIN_R, OFF_COS_A, OFF_SIN_A = 0, 1, 2, 3
GAM_LOG, GAM_CHUNK = 0, 1
RET_GROUP = 8


def _mix_chunk(zhead, rows, not_first, sink_ref, gam_ref, base_refs, off, lane, qn_ref, kn_ref,
               y_ref, state_ref, kprev_ref, vprev_ref):
    cb_r_ref, sb_r_ref, cb_a_ref, sb_a_ref = base_refs
    row = lambda t, k: t[k:k + 1, :]

    cb, sb = cb_r_ref[rows, :], sb_r_ref[rows, :]
    co, so = row(off, OFF_COS_R), row(off, OFF_SIN_R)
    cos_r = cb * co - sb * so
    sin_r = (sb * co + cb * so) * row(lane, LANE_SIGN_R)

    def rot_r(t):
        return t * cos_r + pltpu.roll(t, HEAD_DIM // 2, 1) * sin_r

    n = lax.broadcasted_iota(jnp.int32, (CHUNK, CHUNK), 0)
    m = lax.broadcasted_iota(jnp.int32, (CHUNK, CHUNK), 1)
    causal = n >= m
    lag = jnp.maximum(n - m, 0).astype(F32)
    n_plus_1 = (n + 1).astype(F32)
    to_end = (CHUNK - 1 - n).astype(F32)

    for h0 in range(0, RET_HEADS, RET_GROUP):
        hs = range(h0, h0 + RET_GROUP)
        log_g = [gam_ref[GAM_LOG, h] for h in hs]
        q = [rot_r(zhead("q_r", h)) for h in hs]
        k = [rot_r(zhead("k_r", h)) * (HEAD_DIM ** -0.5) for h in hs]
        vb = [zhead("v_r", h) for h in hs]
        scores = [_dot_nt(q[i].astype(BF16), k[i].astype(BF16))
                  * jnp.where(causal, jnp.exp(lag * log_g[i]), 0.0) for i in range(RET_GROUP)]
        prev = [state_ref[h] for h in hs]
        y = [jnp.dot(jnp.concatenate([scores[i].astype(BF16),
                                      (q[i] * jnp.exp(n_plus_1 * log_g[i])).astype(BF16)], axis=1),
                     jnp.concatenate([vb[i], prev[i].astype(BF16)], axis=0),
                     preferred_element_type=F32) for i in range(RET_GROUP)]
        chunk_kv = [_dot_tn((k[i] * jnp.exp(to_end * log_g[i])).astype(BF16), vb[i])
                    for i in range(RET_GROUP)]
        for i, h in enumerate(hs):
            state_ref[h] = prev[i] * gam_ref[GAM_CHUNK, h] + chunk_kv[i]
        y = [y[i] * lax.rsqrt(jnp.mean(y[i] * y[i], axis=-1, keepdims=True) + EPS)
             for i in range(RET_GROUP)]
        for i, h in enumerate(hs):
            g = zhead("g_r", h)
            y_ref[rows, h * HEAD_DIM:(h + 1) * HEAD_DIM] = (
                (g * jax.nn.sigmoid(g)) * y[i]).astype(BF16)
        yield

    cb, sb = cb_a_ref[rows, :], sb_a_ref[rows, :]
    co, so = row(off, OFF_COS_A), row(off, OFF_SIN_A)
    cos_a = cb * co - sb * so
    sin_a = sb * co + cb * so
    sin_lo = sin_a * row(lane, LANE_MASK_LO)
    sin_hi = sin_a * row(lane, LANE_MASK_HI)
    half = ROPE_DIM // 2

    def rot_a(t):
        return (t * cos_a + pltpu.roll(t, HEAD_DIM - half, 1) * sin_lo
                + pltpu.roll(t, half, 1) * sin_hi)

    qi = lax.broadcasted_iota(jnp.int32, (WINDOW, 2 * WINDOW), 0)
    kj = lax.broadcasted_iota(jnp.int32, (WINDOW, 2 * WINDOW), 1)
    rel = WINDOW + qi - kj
    mask = (rel >= 0) & (rel < WINDOW) & ((kj >= WINDOW) | not_first)
    neg = jnp.finfo(F32).min

    kvs = range(ATT_KV_HEADS)
    all_heads = range(ATT_Q_HEADS)
    kb = [rot_a(_rms_rows(zhead("k_a", kh), kn_ref[...])).astype(BF16) for kh in kvs]
    vb = [zhead("v_a", kh) for kh in kvs]
    kk = [jnp.concatenate([kprev_ref[kh], kb[kh]], axis=0) for kh in kvs]
    vv = [jnp.concatenate([vprev_ref[kh], vb[kh]], axis=0) for kh in kvs]
    qb = [rot_a(_rms_rows(zhead("q_a", qh), qn_ref[...])).astype(BF16) for qh in all_heads]
    s_all = [_dot_nt(jnp.concatenate(qb[kh * GQA_GROUP:(kh + 1) * GQA_GROUP], axis=0), kk[kh])
             * (HEAD_DIM ** -0.5) for kh in kvs]
    sinks = [sink_ref[qh] for qh in all_heads]
    s = [jnp.where(mask, s_all[qh // GQA_GROUP][(qh % GQA_GROUP) * WINDOW:
                                                (qh % GQA_GROUP + 1) * WINDOW, :], neg)
         for qh in all_heads]
    mx = [jnp.maximum(jnp.max(s[qh], axis=-1, keepdims=True), sinks[qh]) for qh in all_heads]
    p = [jnp.exp(s[qh] - mx[qh]) for qh in all_heads]
    inv_denoms = [1.0 / (jnp.sum(p[qh], axis=-1, keepdims=True) + jnp.exp(sinks[qh] - mx[qh]))
                  for qh in all_heads]
    probs = [p[qh].astype(BF16) for qh in all_heads]
    o_all = [jnp.dot(jnp.concatenate(probs[kh * GQA_GROUP:(kh + 1) * GQA_GROUP], axis=0), vv[kh],
                     preferred_element_type=F32) for kh in kvs]
    for qh in all_heads:
        gq = qh % GQA_GROUP
        o = o_all[qh // GQA_GROUP][gq * WINDOW:(gq + 1) * WINDOW, :] * inv_denoms[qh]
        col = RET_WIDTH + qh * HEAD_DIM
        y_ref[rows, col:col + HEAD_DIM] = o.astype(BF16)
    for kh in kvs:
        kprev_ref[kh] = kb[kh]
        vprev_ref[kh] = vb[kh]
    yield


def _mix_layer_kernel(sink_ref, gam_ref, xn_ref, xprev_ref, gain_ref, win_ref, cb_r_ref,
                      sb_r_ref, cb_a_ref, sb_a_ref, off_ref, lane_ref, qn_ref, kn_ref, wout_ref,
                      o_ref, zfa_ref, zva_ref, zfb_ref, zvb_ref, h_ref, y_ref, state_ref,
                      kprev_ref, vprev_ref, *, proj_tile):
    s = pl.program_id(0)
    last = pl.num_programs(0) - 1
    rb = xn_ref.shape[0]
    base_refs = (cb_r_ref, sb_r_ref, cb_a_ref, sb_a_ref)
    z_a = {False: zfa_ref, True: zva_ref}
    z_b = {False: zfb_ref, True: zvb_ref}

    def step(project_into, mix_from):
        pieces = iter(())
        if mix_from is not None:
            off = off_ref[0]
            lane = lane_ref[...]

            def chunk_pieces(cc):
                rows = slice(cc * CHUNK, (cc + 1) * CHUNK)

                def zhead(name, hd):
                    is_value, z_off, _ = _Z_LAYOUT[name]
                    return mix_from[is_value][rows,
                                              z_off + hd * HEAD_DIM: z_off + (hd + 1) * HEAD_DIM]

                not_first = (s > 1) if cc == 0 else True
                return _mix_chunk(zhead, rows, not_first, sink_ref, gam_ref, base_refs, off,
                                  lane, qn_ref, kn_ref, y_ref, state_ref, kprev_ref, vprev_ref)

            pieces = itertools.chain(*[chunk_pieces(cc) for cc in range(rb // CHUNK)])

        if project_into is not None:
            n_tiles = IN_WIDTH // proj_tile
            n_pieces = (rb // CHUNK) * (RET_HEADS // RET_GROUP + 1)
            h_ref[...] = _rms_rows(xn_ref[...], gain_ref[...]).astype(BF16)
            for t in range(n_tiles):
                w_col = t * proj_tile
                is_value, z_col = _z_dest(w_col)
                dst = project_into[is_value]
                tile = jnp.dot(h_ref[...], win_ref[:, w_col:w_col + proj_tile],
                               preferred_element_type=F32)
                dst[:, z_col:z_col + proj_tile] = tile.astype(dst.dtype)
                if (t + 1) * n_pieces // n_tiles > t * n_pieces // n_tiles:
                    next(pieces, None)
        for _ in pieces:
            pass

        if mix_from is not None:
            o_ref[...] = xprev_ref[...] + jnp.dot(y_ref[...], wout_ref[...],
                                                  preferred_element_type=F32)

    @pl.when(s == 0)
    def _():
        state_ref[...] = jnp.zeros_like(state_ref)
        kprev_ref[...] = jnp.zeros_like(kprev_ref)
        vprev_ref[...] = jnp.zeros_like(vprev_ref)
        step(z_b, None)

    @pl.when((s > 0) & (s < last))
    def _():
        step(z_a, z_b)
        for is_value in (False, True):
            z_b[is_value][...] = z_a[is_value][...]

    @pl.when(s == last)
    def _():
        step(None, z_b)


def _mix_layer(x, gain, w_in, w_out, sinks, q_gain, k_gain, *, rb=256, proj_tile=256):
    s, d = x.shape
    nb = s // rb
    t = _position_tables(rb, nb)
    cur = lambda i: (jnp.minimum(i, nb - 1), 0)
    prv = lambda i: (jnp.maximum(i - 1, 0), 0)
    const2 = lambda i: (0, 0)
    resident = dict(pipeline_mode=pl.Buffered(1))
    base_tab = pl.BlockSpec((rb, HEAD_DIM), const2, **resident)
    smem = pl.BlockSpec(memory_space=pltpu.SMEM)
    return pl.pallas_call(
        functools.partial(_mix_layer_kernel, proj_tile=proj_tile),
        name="mix_layer",
        grid=(nb + 1,),
        in_specs=[
            smem, smem,
            pl.BlockSpec((rb, d), cur),
            pl.BlockSpec((rb, d), prv),
            pl.BlockSpec((1, d), const2),
            pl.BlockSpec(w_in.shape, const2, **resident),
            base_tab, base_tab, base_tab, base_tab,
            pl.BlockSpec((1,) + t["block_off"].shape[1:], lambda i: (jnp.maximum(i - 1, 0), 0, 0)),
            pl.BlockSpec(t["lane"].shape, const2),
            pl.BlockSpec((1, HEAD_DIM), const2), pl.BlockSpec((1, HEAD_DIM), const2),
            pl.BlockSpec(w_out.shape, const2, **resident),
        ],
        out_specs=pl.BlockSpec((rb, d), prv),
        out_shape=jax.ShapeDtypeStruct((s, d), F32),
        scratch_shapes=[
            pltpu.VMEM((rb, ZF_WIDTH), F32), pltpu.VMEM((rb, ZV_WIDTH), BF16),
            pltpu.VMEM((rb, ZF_WIDTH), F32), pltpu.VMEM((rb, ZV_WIDTH), BF16),
            pltpu.VMEM((rb, d), BF16),
            pltpu.VMEM((rb, MIX_WIDTH), BF16),
            pltpu.VMEM((RET_HEADS, HEAD_DIM, HEAD_DIM), F32),
            pltpu.VMEM((ATT_KV_HEADS, WINDOW, HEAD_DIM), BF16),
            pltpu.VMEM((ATT_KV_HEADS, WINDOW, HEAD_DIM), BF16),
        ],
        compiler_params=pltpu.CompilerParams(
            dimension_semantics=("arbitrary",),
            vmem_limit_bytes=VMEM_LIMIT_BYTES,
        ),
    )(sinks, t["gamma"], x, x, gain, w_in, t["cos_base_r"], t["sin_base_r"], t["cos_base_a"],
      t["sin_base_a"], t["block_off"], t["lane"], q_gain, k_gain, w_out)


def _position_tables(rb, nb):
    f32 = lambda a: jnp.asarray(np.asarray(a, np.float32))
    r = np.arange(rb, dtype=np.float64)[:, None]
    start = (np.arange(nb, dtype=np.float64) * rb)[:, None]

    inv_r = RET_THETA ** (-np.arange(0, HEAD_DIM, 2, dtype=np.float64) / HEAD_DIM)
    inv_r = np.concatenate([inv_r, inv_r])
    inv_a = ROPE_THETA ** (-np.arange(0, ROPE_DIM, 2, dtype=np.float64) / ROPE_DIM)
    inv_a = np.concatenate([inv_a, inv_a, np.zeros(HEAD_DIM - ROPE_DIM)])

    block_off = np.stack([np.cos(start * inv_r), np.sin(start * inv_r),
                          np.cos(start * inv_a), np.sin(start * inv_a)], axis=1)

    half = ROPE_DIM // 2
    lane = np.zeros((8, HEAD_DIM))
    lane[LANE_SIGN_R] = np.where(np.arange(HEAD_DIM) < HEAD_DIM // 2, -1.0, 1.0)
    lane[LANE_MASK_LO, :half] = -1.0
    lane[LANE_MASK_HI, half:ROPE_DIM] = 1.0

    log_g = np.log1p(-np.exp2(-5.0 - np.arange(RET_HEADS, dtype=np.float64)))
    gamma = np.stack([log_g, np.exp(CHUNK * log_g)])
    return {
        "cos_base_r": f32(np.cos(r * inv_r)), "sin_base_r": f32(np.sin(r * inv_r)),
        "cos_base_a": f32(np.cos(r * inv_a)), "sin_base_a": f32(np.sin(r * inv_a)),
        "block_off": f32(block_off), "lane": f32(lane), "gamma": f32(gamma),
    }


def kernel(x, ffn1_norm, ffn1_w_gate, ffn1_w_up, ffn1_w_down, mix_norm, w_in, q_norm, k_norm,
           attn_sinks, w_out, ffn2_norm, ffn2_w_gate, ffn2_w_up, ffn2_w_down):
    b, s, d = x.shape
    depth = ffn1_norm.shape[0]
    outs = []
    for bi in range(b):
        xb = x[bi]
        for l in range(depth):
            xb, w_in_bf16, w_out_bf16 = _ffn(
                xb, ffn1_norm[l][None], ffn1_w_gate[l], ffn1_w_up[l], ffn1_w_down[l],
                cast=(w_in[l], w_out[l]))
            xb = _mix_layer(xb, mix_norm[l][None], w_in_bf16, w_out_bf16, attn_sinks[l],
                            q_norm[l][None], k_norm[l][None])
            xb, = _ffn(xb, ffn2_norm[l][None], ffn2_w_gate[l], ffn2_w_up[l], ffn2_w_down[l])
        outs.append(xb)
    return jnp.stack(outs, axis=0)
```
